```python
import jax, jax.numpy as jnp
from jax import lax
import numpy as np

D_MODEL = 1024
BATCH = 8
SEQ = 4096
DEPTH = 4

N_MIXERS = 4
EPS = 1e-6
A_CHUNK = 128
A_WIDTH = 2048
A_GROUPS = 4
B_WIDTH = 1024
B_WINDOWS = (2, 4, 8, 16)
B_GROUPS = 4
C_WIDTH = 1024
C_HEADS = 4
C_CONV = 4
C_GATE_C = 8.0
D_WIDTH = 1024
D_CONV = 3
FFN_WIDTH = 2816
FFN_CONV = 3

kernel_name = "hybrid_interleaved_gmlp_pool_rglru_shortconv"


def rms_norm(x, g):
    xf = x.astype(jnp.float32)
    y = xf * lax.rsqrt(jnp.mean(xf * xf, axis=-1, keepdims=True) + EPS)
    return (y * g.astype(jnp.float32)).astype(x.dtype)


def causal_dwconv(x, w):
    k_taps = w.shape[0]
    s = x.shape[1]
    xp = jnp.pad(x, ((0, 0), (k_taps - 1, 0), (0, 0)))
    y = xp[:, 0:s] * w[0]
    for k in range(1, k_taps):
        y = y + xp[:, k:k + s] * w[k]
    return y


def mixer_gmlp(h, w_in, b_in, v_norm_g, w_s, b_s, w_out):
    bsz, s, _ = h.shape
    z = jax.nn.gelu(h @ w_in + b_in)
    u, v = jnp.split(z, 2, axis=-1)
    v = rms_norm(v, v_norm_g)
    gw = A_WIDTH // A_GROUPS
    v = v.reshape(bsz, s // A_CHUNK, A_CHUNK, A_GROUPS, gw)
    mask = jnp.tril(jnp.ones((A_CHUNK, A_CHUNK), dtype=bool))
    ws = jnp.where(mask, w_s, jnp.zeros_like(w_s))
    v = jnp.einsum('gts,bcsgd->bctgd', ws, v) + b_s.T[:, :, None]
    v = v.reshape(bsz, s, A_WIDTH)
    return (u * v) @ w_out


def mixer_pool(h, w_in, w_grp, b_grp, scale, w_out):
    bsz, s, _ = h.shape
    z = h @ w_in
    gw = B_WIDTH // B_GROUPS
    zf = z.astype(jnp.float32)
    csum = jnp.pad(jnp.cumsum(zf, axis=1), ((0, 0), (1, 0), (0, 0)))
    pos = jnp.arange(s)
    pooled = []
    for gi, win in enumerate(B_WINDOWS):
        c = csum[..., gi * gw:(gi + 1) * gw]
        upper = c[:, 1:]
        lower = jnp.pad(c[:, :s + 1 - win], ((0, 0), (win - 1, 0), (0, 0)))
        cnt = jnp.minimum(pos + 1, win).astype(jnp.float32)
        pooled.append((upper - lower) / cnt[None, :, None])
    p = (jnp.concatenate(pooled, axis=-1) - zf).astype(z.dtype)
    p = p.reshape(bsz, s, B_GROUPS, gw)
    y = jnp.einsum('bsgd,gde->bsge', p, w_grp) + b_grp
    y = y.reshape(bsz, s, B_WIDTH) * scale
    return y @ w_out


def mixer_rglru(h, w_in, b_in, conv_w, conv_b, w_a, b_a, w_i, b_i, lam, w_out):
    bsz, s, _ = h.shape
    z = h @ w_in + b_in
    gate, xr = jnp.split(z, 2, axis=-1)
    gate = jax.nn.gelu(gate)
    xr = causal_dwconv(xr, conv_w) + conv_b
    hw = C_WIDTH // C_HEADS
    xh = xr.reshape(bsz, s, C_HEADS, hw)
    r = jax.nn.sigmoid(jnp.einsum('bshd,hde->bshe', xh, w_a) + b_a).reshape(bsz, s, C_WIDTH)
    ig = jax.nn.sigmoid(jnp.einsum('bshd,hde->bshe', xh, w_i) + b_i).reshape(bsz, s, C_WIDTH)
    log_a = (-C_GATE_C * r.astype(jnp.float32)) * jax.nn.softplus(-lam.astype(jnp.float32))
    a = jnp.exp(log_a)
    mult = jnp.sqrt(-jnp.expm1(2.0 * log_a))
    bterm = mult * (ig * xr).astype(jnp.float32)

    def combine(left, right):
        a1, b1 = left
        a2, b2 = right
        return a1 * a2, a2 * b1 + b2

    _, hs = lax.associative_scan(combine, (a, bterm), axis=1)
    y = hs.astype(h.dtype) * gate
    return y @ w_out


def mixer_shortconv(h, w_in, conv_w, w_out):
    z = h @ w_in
    b_gate, c_gate, xv = jnp.split(z, 3, axis=-1)
    y = b_gate * causal_dwconv(c_gate * xv, conv_w)
    return y @ w_out


def conv_ffn(h, w_up, conv_w, conv_b, w_down):
    z = causal_dwconv(h @ w_up, conv_w) + conv_b
    g, v = jnp.split(z, 2, axis=-1)
    return (jax.nn.silu(g) * v) @ w_down


def setup_inputs(seed: int = 0) -> dict:
    key = jax.random.key(seed)
    keys = iter(jax.random.split(key, 64))
    n_a, n_b, n_c, n_d = (len(range(m, DEPTH, N_MIXERS)) for m in range(N_MIXERS))
    d = D_MODEL

    def nrm(shape, scale):
        return scale * jax.random.normal(next(keys), shape, jnp.float32)

    def gain(shape):
        return 1.0 + nrm(shape, 0.02)

    inp = {}
    inp["x"] = nrm((BATCH, SEQ, d), 1.0)
    inp["a_norm_g"] = gain((n_a, d))
    inp["a_w_in"] = nrm((n_a, d, 2 * A_WIDTH), d ** -0.5)
    inp["a_b_in"] = nrm((n_a, 2 * A_WIDTH), 0.02)
    inp["a_v_norm_g"] = gain((n_a, A_WIDTH))
    inp["a_w_s"] = nrm((n_a, A_GROUPS, A_CHUNK, A_CHUNK), A_CHUNK ** -0.5)
    inp["a_b_s"] = 1.0 + nrm((n_a, A_GROUPS, A_CHUNK), 0.1)
    inp["a_w_out"] = nrm((n_a, A_WIDTH, d), 0.5 * A_WIDTH ** -0.5)
    gwb = B_WIDTH // B_GROUPS
    inp["b_norm_g"] = gain((n_b, d))
    inp["b_w_in"] = nrm((n_b, d, B_WIDTH), d ** -0.5)
    inp["b_w_grp"] = nrm((n_b, B_GROUPS, gwb, gwb), gwb ** -0.5)
    inp["b_b_grp"] = nrm((n_b, B_GROUPS, gwb), 0.02)
    inp["b_scale"] = 1.0 + nrm((n_b, B_WIDTH), 0.1)
    inp["b_w_out"] = nrm((n_b, B_WIDTH, d), 0.5 * B_WIDTH ** -0.5)
    hwc = C_WIDTH // C_HEADS
    inp["c_norm_g"] = gain((n_c, d))
    inp["c_w_in"] = nrm((n_c, d, 2 * C_WIDTH), d ** -0.5)
    inp["c_b_in"] = nrm((n_c, 2 * C_WIDTH), 0.02)
    inp["c_conv_w"] = nrm((n_c, C_CONV, C_WIDTH), C_CONV ** -0.5)
    inp["c_conv_b"] = nrm((n_c, C_WIDTH), 0.02)
    inp["c_w_a"] = nrm((n_c, C_HEADS, hwc, hwc), hwc ** -0.5)
    inp["c_b_a"] = nrm((n_c, C_HEADS, hwc), 0.02)
    inp["c_w_i"] = nrm((n_c, C_HEADS, hwc, hwc), hwc ** -0.5)
    inp["c_b_i"] = nrm((n_c, C_HEADS, hwc), 0.02)
    u = jax.random.uniform(next(keys), (n_c, C_WIDTH), jnp.float32, minval=0.9, maxval=0.999)
    sgm = u ** (1.0 / C_GATE_C)
    inp["c_lambda"] = jnp.log(sgm) - jnp.log1p(-sgm)
    inp["c_w_out"] = nrm((n_c, C_WIDTH, d), C_WIDTH ** -0.5)
    inp["d_norm_g"] = gain((n_d, d))
    inp["d_w_in"] = nrm((n_d, d, 3 * D_WIDTH), d ** -0.5)
    inp["d_conv_w"] = nrm((n_d, D_CONV, D_WIDTH), D_CONV ** -0.5)
    inp["d_w_out"] = nrm((n_d, D_WIDTH, d), 0.5 * D_WIDTH ** -0.5)
    inp["ffn_norm_g"] = gain((DEPTH, d))
    inp["ffn_w_up"] = nrm((DEPTH, d, 2 * FFN_WIDTH), d ** -0.5)
    inp["ffn_conv_w"] = nrm((DEPTH, FFN_CONV, 2 * FFN_WIDTH), FFN_CONV ** -0.5)
    inp["ffn_conv_b"] = nrm((DEPTH, 2 * FFN_WIDTH), 0.02)
    inp["ffn_w_down"] = nrm((DEPTH, FFN_WIDTH, d), FFN_WIDTH ** -0.5)
    inp["final_norm_g"] = gain((d,))
    return inp


def reference(x,
              a_norm_g, a_w_in, a_b_in, a_v_norm_g, a_w_s, a_b_s, a_w_out,
              b_norm_g, b_w_in, b_w_grp, b_b_grp, b_scale, b_w_out,
              c_norm_g, c_w_in, c_b_in, c_conv_w, c_conv_b, c_w_a, c_b_a, c_w_i, c_b_i, c_lambda, c_w_out,
              d_norm_g, d_w_in, d_conv_w, d_w_out,
              ffn_norm_g, ffn_w_up, ffn_conv_w, ffn_conv_b, ffn_w_down,
              final_norm_g):
    for layer in range(DEPTH):
        m, j = layer % N_MIXERS, layer // N_MIXERS
        if m == 0:
            x = x + mixer_gmlp(rms_norm(x, a_norm_g[j]), a_w_in[j], a_b_in[j], a_v_norm_g[j],
                               a_w_s[j], a_b_s[j], a_w_out[j])
        elif m == 1:
            x = x + mixer_pool(rms_norm(x, b_norm_g[j]), b_w_in[j], b_w_grp[j], b_b_grp[j],
                               b_scale[j], b_w_out[j])
        elif m == 2:
            x = x + mixer_rglru(rms_norm(x, c_norm_g[j]), c_w_in[j], c_b_in[j], c_conv_w[j], c_conv_b[j],
                                c_w_a[j], c_b_a[j], c_w_i[j], c_b_i[j], c_lambda[j], c_w_out[j])
        else:
            x = x + mixer_shortconv(rms_norm(x, d_norm_g[j]), d_w_in[j], d_conv_w[j], d_w_out[j])
        x = x + conv_ffn(rms_norm(x, ffn_norm_g[layer]), ffn_w_up[layer], ffn_conv_w[layer],
                         ffn_conv_b[layer], ffn_w_down[layer])
    return rms_norm(x, final_norm_g)
```

```python
import functools

import jax
import jax.numpy as jnp
from jax import lax
from jax.experimental import pallas as pl
from jax.experimental.pallas import tpu as pltpu

EPS = 1e-6
SUBLANES = 8
C_GATE_C = 8.0
B_WINDOWS = (2, 4, 8, 16)
A_CHUNK = 128
VMEM_LIMIT = 56 * 1024 * 1024

F32 = jnp.float32
BF16 = jnp.bfloat16


def _rms(x, g):
    ms = jnp.mean(x * x, axis=-1, keepdims=True)
    return x * lax.rsqrt(ms + EPS) * g


def _gelu(x):
    c = 0.7978845608028654
    return 0.5 * x * (1.0 + jnp.tanh(c * (x + 0.044715 * (x * x * x))))


def _dot(a, b):
    return jnp.dot(a, b, preferred_element_type=F32)


def _const_spec(shape):
    nd = len(shape)
    return pl.BlockSpec(shape, lambda b, i: (0,) * nd, pipeline_mode=pl.Buffered(1))


def _tile_spec(tm, d):
    return pl.BlockSpec((1, tm, d), lambda b, i: (b, i, 0))


def _call(body, x, consts, tm, scratch, name):
    bsz, s, d = x.shape
    return pl.pallas_call(
        body,
        grid=(bsz, s // tm),
        in_specs=[_tile_spec(tm, d)] + [_const_spec(c.shape) for c in consts],
        out_specs=_tile_spec(tm, d),
        out_shape=jax.ShapeDtypeStruct(x.shape, x.dtype),
        scratch_shapes=scratch,
        compiler_params=pltpu.CompilerParams(
            dimension_semantics=("arbitrary", "arbitrary"), vmem_limit_bytes=VMEM_LIMIT),
        name=name,
    )(x, *consts)


def _ffn_body(x_ref, g_ref, wup_ref, cw_ref, cb_ref, wdn_ref, fg_ref, o_ref, zbuf, ybuf, *,
              tm, rb, f, cw, final):
    halo = SUBLANES

    @pl.when(pl.program_id(1) == 0)
    def _():
        zbuf[0:halo, :] = jnp.zeros((halo, 2 * f), F32)

    for r0 in range(0, tm, rb):
        x = x_ref[0, r0:r0 + rb, :]
        h = _rms(x, g_ref[...]).astype(BF16)
        zbuf[halo + r0:halo + r0 + rb, :] = _dot(h, wup_ref[...])
        for c0 in range(0, f, cw):
            def conv(col):
                w = cw_ref[:, col:col + cw]
                z2 = zbuf[halo + r0 - 2:halo + r0 - 2 + rb, col:col + cw]
                z1 = zbuf[halo + r0 - 1:halo + r0 - 1 + rb, col:col + cw]
                z0 = zbuf[halo + r0:halo + r0 + rb, col:col + cw]
                return z2 * w[0:1] + z1 * w[1:2] + z0 * w[2:3] + cb_ref[:, col:col + cw]
            gt = conv(c0)
            vl = conv(f + c0)
            ybuf[r0:r0 + rb, c0:c0 + cw] = ((gt * jax.nn.sigmoid(gt)) * vl).astype(BF16)
        out = x + _dot(ybuf[r0:r0 + rb, :], wdn_ref[...])
        if final:
            out = _rms(out, fg_ref[...])
        o_ref[0, r0:r0 + rb, :] = out
    zbuf[0:halo, :] = zbuf[tm:tm + halo, :]


def _ffn(x, g, w_up, conv_w, conv_b, w_down, final_g, final, tm=512, rb=256, cw=256):
    d = x.shape[-1]
    f = w_down.shape[0]
    consts = [g.reshape(1, d), w_up.astype(BF16), conv_w, conv_b.reshape(1, 2 * f),
              w_down.astype(BF16), final_g.reshape(1, d)]
    body = functools.partial(_ffn_body, tm=tm, rb=rb, f=f, cw=cw, final=final)
    scratch = [pltpu.VMEM((tm + SUBLANES, 2 * f), F32), pltpu.VMEM((tm, f), BF16)]
    return _call(body, x, consts, tm, scratch, "ffn_final" if final else "ffn")


def _gmlp_body(x_ref, g_ref, win_ref, bin_ref, vg_ref, ws_ref, bs_ref, wout_ref, o_ref, *, tm, aw, groups):
    x = x_ref[0]
    h = _rms(x, g_ref[...]).astype(BF16)
    zv = _gelu(_dot(h, win_ref[:, aw:2 * aw]) + bin_ref[:, aw:2 * aw])
    v = _rms(zv, vg_ref[...]).astype(BF16)
    gw = aw // groups
    t_idx = lax.broadcasted_iota(jnp.int32, (A_CHUNK, A_CHUNK), 0)
    s_idx = lax.broadcasted_iota(jnp.int32, (A_CHUNK, A_CHUNK), 1)
    rows = []
    for k0 in range(0, tm, A_CHUNK):
        cols = []
        for gi in range(groups):
            ws = jnp.where(s_idx <= t_idx, ws_ref[gi], 0.0).astype(BF16)
            vm = _dot(ws, v[k0:k0 + A_CHUNK, gi * gw:(gi + 1) * gw]) + bs_ref[gi]
            cols.append(vm)
        rows.append(jnp.concatenate(cols, axis=1))
    vm = jnp.concatenate(rows, axis=0) if len(rows) > 1 else rows[0]
    zu = _gelu(_dot(h, win_ref[:, 0:aw]) + bin_ref[:, 0:aw])
    o_ref[0] = x + _dot((zu * vm).astype(BF16), wout_ref[...])


def _gmlp(x, g, w_in, b_in, vg, w_s, b_s, w_out, tm=256):
    d = x.shape[-1]
    aw = w_out.shape[0]
    groups = w_s.shape[0]
    consts = [g.reshape(1, d), w_in.astype(BF16), b_in.reshape(1, 2 * aw), vg.reshape(1, aw),
              w_s, b_s[..., None], w_out.astype(BF16)]
    body = functools.partial(_gmlp_body, tm=tm, aw=aw, groups=groups)
    return _call(body, x, consts, tm, [], "mixer_gmlp")


def _pool_body(x_ref, g_ref, win_ref, wg_ref, bg_ref, sc_ref, wout_ref, o_ref, zbuf, *, tm, bw, groups):
    halo = 2 * SUBLANES
    i = pl.program_id(1)

    @pl.when(i == 0)
    def _():
        zbuf[0:halo, :] = jnp.zeros((halo, bw), F32)

    x = x_ref[0]
    h = _rms(x, g_ref[...]).astype(BF16)
    z = _dot(h, win_ref[...])
    zbuf[halo:halo + tm, :] = z
    gw = bw // groups
    pos = i * tm + lax.broadcasted_iota(jnp.int32, (tm, 1), 0)
    ys = []
    for gi, win in enumerate(B_WINDOWS):
        cs = slice(gi * gw, (gi + 1) * gw)
        acc = z[:, cs]
        for k in range(1, win):
            acc = acc + zbuf[halo - k:halo - k + tm, cs]
        inv_cnt = 1.0 / jnp.minimum(pos + 1, win).astype(F32)
        p = (acc * inv_cnt - z[:, cs]).astype(BF16)
        ys.append(_dot(p, wg_ref[gi].astype(BF16)) + bg_ref[gi:gi + 1, :])
    y = jnp.concatenate(ys, axis=1) * sc_ref[...]
    o_ref[0] = x + _dot(y.astype(BF16), wout_ref[...])
    zbuf[0:halo, :] = zbuf[tm:tm + halo, :]


def _pool(x, g, w_in, w_grp, b_grp, scale, w_out, tm=256):
    d = x.shape[-1]
    bw = w_out.shape[0]
    groups = w_grp.shape[0]
    consts = [g.reshape(1, d), w_in.astype(BF16), w_grp, b_grp, scale.reshape(1, bw), w_out.astype(BF16)]
    body = functools.partial(_pool_body, tm=tm, bw=bw, groups=groups)
    scratch = [pltpu.VMEM((tm + 2 * SUBLANES, bw), F32)]
    return _call(body, x, consts, tm, scratch, "mixer_pool")


def _rglru_body(x_ref, g_ref, win_ref, bin_ref, cw_ref, cb_ref, wa_ref, ba_ref, wi_ref, bi_ref, lam_ref,
                wout_ref, o_ref, xbuf, abuf, bbuf, hbuf, hcar, *, tm, cwid, heads, taps):
    halo = SUBLANES

    @pl.when(pl.program_id(1) == 0)
    def _():
        xbuf[0:halo, :] = jnp.zeros((halo, cwid), F32)
        hcar[...] = jnp.zeros((SUBLANES, cwid), F32)

    x = x_ref[0]
    h = _rms(x, g_ref[...]).astype(BF16)
    gate = _gelu(_dot(h, win_ref[:, 0:cwid]) + bin_ref[:, 0:cwid])
    xbuf[halo:halo + tm, :] = _dot(h, win_ref[:, cwid:2 * cwid]) + bin_ref[:, cwid:2 * cwid]
    xr = cb_ref[...] + xbuf[halo:halo + tm, :] * cw_ref[taps - 1:taps, :]
    for k in range(1, taps):
        xr = xr + xbuf[halo - k:halo - k + tm, :] * cw_ref[taps - 1 - k:taps - k, :]
    hw = cwid // heads
    xrb = xr.astype(BF16)
    rs, igs = [], []
    for hd in range(heads):
        cs = slice(hd * hw, (hd + 1) * hw)
        rs.append(_dot(xrb[:, cs], wa_ref[hd].astype(BF16)) + ba_ref[hd:hd + 1, :])
        igs.append(_dot(xrb[:, cs], wi_ref[hd].astype(BF16)) + bi_ref[hd:hd + 1, :])
    r = jax.nn.sigmoid(jnp.concatenate(rs, axis=1))
    ig = jax.nn.sigmoid(jnp.concatenate(igs, axis=1))
    nlam = -lam_ref[...]
    softplus = jnp.maximum(nlam, 0.0) + jnp.log1p(jnp.exp(-jnp.abs(nlam)))
    log_a = (-C_GATE_C * r) * softplus
    abuf[...] = jnp.exp(log_a)
    th = jnp.tanh(log_a)
    bbuf[...] = jnp.sqrt(-2.0 * th / (1.0 - th)) * (ig * xr)

    row = lax.broadcasted_iota(jnp.int32, (SUBLANES, cwid), 0)

    def step(j, carry):
        r0 = pl.multiple_of(j * SUBLANES, SUBLANES)
        a = abuf[pl.ds(r0, SUBLANES), :]
        b = bbuf[pl.ds(r0, SUBLANES), :]
        for k in (1, 2, 4):
            a_sh = jnp.where(row < k, 1.0, pltpu.roll(a, k, 0))
            b_sh = jnp.where(row < k, 0.0, pltpu.roll(b, k, 0))
            b = a * b_sh + b
            a = a * a_sh
        hs = a * carry + b
        hbuf[pl.ds(r0, SUBLANES), :] = hs
        return jnp.broadcast_to(hs[SUBLANES - 1:SUBLANES, :], (SUBLANES, cwid))

    hcar[...] = lax.fori_loop(0, tm // SUBLANES, step, hcar[...])
    y = (hbuf[...] * gate).astype(BF16)
    o_ref[0] = x + _dot(y, wout_ref[...])
    xbuf[0:halo, :] = xbuf[tm:tm + halo, :]


def _rglru(x, g, w_in, b_in, conv_w, conv_b, w_a, b_a, w_i, b_i, lam, w_out, tm=256):
    d = x.shape[-1]
    cwid = w_out.shape[0]
    heads = w_a.shape[0]
    taps = conv_w.shape[0]
    consts = [g.reshape(1, d), w_in.astype(BF16), b_in.reshape(1, 2 * cwid), conv_w, conv_b.reshape(1, cwid),
              w_a, b_a, w_i, b_i, lam.reshape(1, cwid), w_out.astype(BF16)]
    body = functools.partial(_rglru_body, tm=tm, cwid=cwid, heads=heads, taps=taps)
    scratch = [pltpu.VMEM((tm + SUBLANES, cwid), F32), pltpu.VMEM((tm, cwid), F32), pltpu.VMEM((tm, cwid), F32),
               pltpu.VMEM((tm, cwid), F32), pltpu.VMEM((SUBLANES, cwid), F32)]
    return _call(body, x, consts, tm, scratch, "mixer_rglru")


def _sconv_body(x_ref, g_ref, win_ref, cw_ref, wout_ref, o_ref, mbuf, *, tm, dw, taps):
    halo = SUBLANES

    @pl.when(pl.program_id(1) == 0)
    def _():
        mbuf[0:halo, :] = jnp.zeros((halo, dw), F32)

    x = x_ref[0]
    h = _rms(x, g_ref[...]).astype(BF16)
    bg = _dot(h, win_ref[:, 0:dw])
    mbuf[halo:halo + tm, :] = _dot(h, win_ref[:, dw:2 * dw]) * _dot(h, win_ref[:, 2 * dw:3 * dw])
    cv = mbuf[halo:halo + tm, :] * cw_ref[taps - 1:taps, :]
    for k in range(1, taps):
        cv = cv + mbuf[halo - k:halo - k + tm, :] * cw_ref[taps - 1 - k:taps - k, :]
    o_ref[0] = x + _dot((bg * cv).astype(BF16), wout_ref[...])
    mbuf[0:halo, :] = mbuf[tm:tm + halo, :]


def _sconv(x, g, w_in, conv_w, w_out, tm=256):
    d = x.shape[-1]
    dw = w_out.shape[0]
    taps = conv_w.shape[0]
    consts = [g.reshape(1, d), w_in.astype(BF16), conv_w, w_out.astype(BF16)]
    body = functools.partial(_sconv_body, tm=tm, dw=dw, taps=taps)
    scratch = [pltpu.VMEM((tm + SUBLANES, dw), F32)]
    return _call(body, x, consts, tm, scratch, "mixer_sconv")


def kernel(x, a_norm_g, a_w_in, a_b_in, a_v_norm_g, a_w_s, a_b_s, a_w_out, b_norm_g, b_w_in, b_w_grp, b_b_grp, b_scale, b_w_out, c_norm_g, c_w_in, c_b_in, c_conv_w, c_conv_b, c_w_a, c_b_a, c_w_i, c_b_i, c_lambda, c_w_out, d_norm_g, d_w_in, d_conv_w, d_w_out, ffn_norm_g, ffn_w_up, ffn_conv_w, ffn_conv_b, ffn_w_down, final_norm_g):
    depth = ffn_norm_g.shape[0]
    n_mixers = 4
    for layer in range(depth):
        m, j = layer % n_mixers, layer // n_mixers
        if m == 0:
            x = _gmlp(x, a_norm_g[j], a_w_in[j], a_b_in[j], a_v_norm_g[j], a_w_s[j], a_b_s[j], a_w_out[j])
        elif m == 1:
            x = _pool(x, b_norm_g[j], b_w_in[j], b_w_grp[j], b_b_grp[j], b_scale[j], b_w_out[j])
        elif m == 2:
            x = _rglru(x, c_norm_g[j], c_w_in[j], c_b_in[j], c_conv_w[j], c_conv_b[j], c_w_a[j], c_b_a[j],
                       c_w_i[j], c_b_i[j], c_lambda[j], c_w_out[j])
        else:
            x = _sconv(x, d_norm_g[j], d_w_in[j], d_conv_w[j], d_w_out[j])
        x = _ffn(x, ffn_norm_g[layer], ffn_w_up[layer], ffn_conv_w[layer], ffn_conv_b[layer],
                 ffn_w_down[layer], final_norm_g, final=(layer == depth - 1))
    return x
```

```python
import functools

import jax
import jax.numpy as jnp
from jax import lax
from jax.experimental import pallas as pl
from jax.experimental.pallas import tpu as pltpu

EPS = 1e-6
SUB = 8
PB = 256
L = PB // SUB
C_GATE_C = 8.0
B_WINDOWS = (2, 4, 8, 16)
A_CHUNK = 128
VMEM_LIMIT = 56 * 1024 * 1024

F32 = jnp.float32
BF16 = jnp.bfloat16


def _to_perm(x):
    b, s, d = x.shape
    return x.reshape(b, s // PB, SUB, L, d).transpose(0, 1, 3, 2, 4).reshape(b, s, d)


def _from_perm(x):
    b, s, d = x.shape
    return x.reshape(b, s // PB, L, SUB, d).transpose(0, 1, 3, 2, 4).reshape(b, s, d)


def _block_time(shape, axis):
    p = lax.broadcasted_iota(jnp.int32, shape, axis)
    return (p & (SUB - 1)) * L + (p >> 3)


def _rms(x, g):
    ms = jnp.mean(x * x, axis=-1, keepdims=True)
    return x * lax.rsqrt(ms + EPS) * g


def _gelu(x):
    c = 0.7978845608028654
    return 0.5 * x * (1.0 + jnp.tanh(c * (x + 0.044715 * (x * x * x))))


def _dot(a, b):
    return jnp.dot(a, b, preferred_element_type=F32)


def _roll_tiles(t, k=1):
    parts = [pltpu.roll(t[r:r + SUB], k, 0) for r in range(0, t.shape[0], SUB)]
    return parts[0] if len(parts) == 1 else jnp.concatenate(parts, axis=0)


def _halo(cur_tail, prev_tail):
    first = (lax.broadcasted_iota(jnp.int32, cur_tail.shape, 0) & (SUB - 1)) == 0
    return jnp.where(first, _roll_tiles(prev_tail), _roll_tiles(cur_tail))


def _const_spec(shape):
    nd = len(shape)
    return pl.BlockSpec(shape, lambda b, i: (0,) * nd, pipeline_mode=pl.Buffered(1))


def _tile_spec(tm, d):
    return pl.BlockSpec((1, tm, d), lambda b, i: (b, i, 0))


def _call(body, x, consts, tm, scratch, name):
    bsz, s, d = x.shape
    return pl.pallas_call(
        body,
        grid=(bsz, s // tm),
        in_specs=[_tile_spec(tm, d)] + [_const_spec(c.shape) for c in consts],
        out_specs=_tile_spec(tm, d),
        out_shape=jax.ShapeDtypeStruct(x.shape, x.dtype),
        scratch_shapes=scratch,
        compiler_params=pltpu.CompilerParams(
            dimension_semantics=("arbitrary", "arbitrary"), vmem_limit_bytes=VMEM_LIMIT),
        name=name,
    )(x, *consts)


def _pair_chunks(w, f, cw):
    lead = w.shape[:-1]
    return w.reshape(*lead, 2, f // cw, cw).swapaxes(-3, -2).reshape(*lead, 2 * f)


def _ffn_body(x_ref, g_ref, wup_ref, cw_ref, cb_ref, wdn_ref, fg_ref, o_ref, zbuf, ybuf, tail, *,
              nb, f, cw, final):
    hr = 2 * SUB

    @pl.when(pl.program_id(1) == 0)
    def _():
        tail[...] = jnp.zeros(tail.shape, F32)

    for n in range(nb):
        r0 = n * PB
        x = x_ref[0, r0:r0 + PB, :]
        h = _rms(x, g_ref[...]).astype(BF16)
        zbuf[n, hr:hr + PB, :] = _dot(h, wup_ref[...])
        for c in range(f // cw):
            cs = slice(2 * c * cw, 2 * (c + 1) * cw)
            cur = zbuf[n, PB:PB + hr, cs]
            zbuf[n, 0:hr, cs] = _halo(cur, tail[:, cs])
            tail[:, cs] = cur
            w = cw_ref[:, cs]
            zc = (zbuf[n, 0:PB, cs] * w[0:1] + zbuf[n, SUB:SUB + PB, cs] * w[1:2]
                  + zbuf[n, hr:hr + PB, cs] * w[2:3] + cb_ref[:, cs])
            gt = zc[:, :cw]
            ybuf[r0:r0 + PB, c * cw:(c + 1) * cw] = ((gt * jax.nn.sigmoid(gt)) * zc[:, cw:]).astype(BF16)
        out = x + _dot(ybuf[r0:r0 + PB, :], wdn_ref[...])
        if final:
            out = _rms(out, fg_ref[...])
        o_ref[0, r0:r0 + PB, :] = out


def _ffn(x, g, w_up, conv_w, conv_b, w_down, final_g, final, tm=512, cw=256):
    d = x.shape[-1]
    f = w_down.shape[0]
    nb = tm // PB
    consts = [g.reshape(1, d), _pair_chunks(w_up, f, cw).astype(BF16), _pair_chunks(conv_w, f, cw),
              _pair_chunks(conv_b, f, cw).reshape(1, 2 * f), w_down.astype(BF16), final_g.reshape(1, d)]
    body = functools.partial(_ffn_body, nb=nb, f=f, cw=cw, final=final)
    scratch = [pltpu.VMEM((nb, PB + 2 * SUB, 2 * f), F32), pltpu.VMEM((tm, f), BF16),
               pltpu.VMEM((2 * SUB, 2 * f), F32)]
    return _call(body, x, consts, tm, scratch, "ffn_final" if final else "ffn")


def _gmlp_body(x_ref, g_ref, win_ref, bin_ref, vg_ref, ws_ref, bs_ref, wout_ref, o_ref, *, nb, aw, groups):
    gw = aw // groups
    tp = _block_time((PB, PB), 0)
    tq = _block_time((PB, PB), 1)
    chunk_bits = A_CHUNK.bit_length() - 1
    keep = (tq <= tp) & ((tq >> chunk_bits) == (tp >> chunk_bits))
    for n in range(nb):
        r0 = n * PB
        x = x_ref[0, r0:r0 + PB, :]
        h = _rms(x, g_ref[...]).astype(BF16)
        zv = _gelu(_dot(h, win_ref[:, aw:2 * aw]) + bin_ref[:, aw:2 * aw])
        v = _rms(zv, vg_ref[...]).astype(BF16)
        cols = []
        for gi in range(groups):
            ws = jnp.where(keep, ws_ref[gi], 0.0).astype(BF16)
            cols.append(_dot(ws, v[:, gi * gw:(gi + 1) * gw]) + bs_ref[gi])
        vm = jnp.concatenate(cols, axis=1)
        zu = _gelu(_dot(h, win_ref[:, 0:aw]) + bin_ref[:, 0:aw])
        o_ref[0, r0:r0 + PB, :] = x + _dot((zu * vm).astype(BF16), wout_ref[...])


def _gmlp(x, g, w_in, b_in, vg, w_s, b_s, w_out, tm=512):
    d = x.shape[-1]
    aw = w_out.shape[0]
    groups = w_s.shape[0]
    p = jnp.arange(PB)
    idx = ((p % SUB) * L + p // SUB) % A_CHUNK
    ws_p = w_s[:, idx][:, :, idx]
    bs_p = b_s[:, idx][..., None]
    consts = [g.reshape(1, d), w_in.astype(BF16), b_in.reshape(1, 2 * aw), vg.reshape(1, aw),
              ws_p, bs_p, w_out.astype(BF16)]
    body = functools.partial(_gmlp_body, nb=tm // PB, aw=aw, groups=groups)
    return _call(body, x, consts, tm, [], "mixer_gmlp")


def _pool_body(x_ref, g_ref, win_ref, wg_ref, bg_ref, sc_ref, wout_ref, o_ref, tail, *, nb, bw, groups):
    hr = 16 * SUB
    i = pl.program_id(1)

    @pl.when(i == 0)
    def _():
        tail[...] = jnp.zeros(tail.shape, F32)

    gw = bw // groups
    for n in range(nb):
        r0 = n * PB
        x = x_ref[0, r0:r0 + PB, :]
        h = _rms(x, g_ref[...]).astype(BF16)
        z = _dot(h, win_ref[...])
        cur = z[PB - hr:PB, :]
        ext = jnp.concatenate([_halo(cur, tail[...]), z], axis=0)
        tail[...] = cur
        pos = (i * nb + n) * PB + _block_time((PB, 1), 0)
        ys = []
        for gi, win in enumerate(B_WINDOWS):
            cs = slice(gi * gw, (gi + 1) * gw)
            s = ext[:, cs]
            step = 1
            while step < win:
                s = s[step * SUB:] + s[:-step * SUB]
                step *= 2
            acc = s[s.shape[0] - PB:]
            inv_cnt = 1.0 / jnp.minimum(pos + 1, win).astype(F32)
            p = (acc * inv_cnt - z[:, cs]).astype(BF16)
            ys.append(_dot(p, wg_ref[gi].astype(BF16)) + bg_ref[gi:gi + 1, :])
        y = jnp.concatenate(ys, axis=1) * sc_ref[...]
        o_ref[0, r0:r0 + PB, :] = x + _dot(y.astype(BF16), wout_ref[...])


def _pool(x, g, w_in, w_grp, b_grp, scale, w_out, tm=512):
    d = x.shape[-1]
    bw = w_out.shape[0]
    groups = w_grp.shape[0]
    consts = [g.reshape(1, d), w_in.astype(BF16), w_grp, b_grp, scale.reshape(1, bw), w_out.astype(BF16)]
    body = functools.partial(_pool_body, nb=tm // PB, bw=bw, groups=groups)
    scratch = [pltpu.VMEM((16 * SUB, bw), F32)]
    return _call(body, x, consts, tm, scratch, "mixer_pool")


def _rglru_body(x_ref, g_ref, win_ref, bin_ref, cw_ref, cb_ref, wa_ref, ba_ref, wi_ref, bi_ref, lam_ref,
                wout_ref, o_ref, xbuf, abuf, hbuf, tail, hcar, *, nb, cwid, heads, taps):
    hr = (taps - 1) * SUB

    @pl.when(pl.program_id(1) == 0)
    def _():
        tail[...] = jnp.zeros(tail.shape, F32)
        hcar[...] = jnp.zeros(hcar.shape, F32)

    hw = cwid // heads
    sub = lax.broadcasted_iota(jnp.int32, (SUB, cwid), 0)
    nlam = -lam_ref[...]
    softplus = jnp.maximum(nlam, 0.0) + jnp.log1p(jnp.exp(-jnp.abs(nlam)))
    for n in range(nb):
        r0 = n * PB
        x = x_ref[0, r0:r0 + PB, :]
        h = _rms(x, g_ref[...]).astype(BF16)
        gate = _gelu(_dot(h, win_ref[:, 0:cwid]) + bin_ref[:, 0:cwid])
        xbuf[hr:hr + PB, :] = _dot(h, win_ref[:, cwid:2 * cwid]) + bin_ref[:, cwid:2 * cwid]
        cur = xbuf[PB:PB + hr, :]
        xbuf[0:hr, :] = _halo(cur, tail[...])
        tail[...] = cur
        xr = cb_ref[...] + xbuf[hr:hr + PB, :] * cw_ref[taps - 1:taps, :]
        for k in range(1, taps):
            xr = xr + xbuf[hr - k * SUB:hr - k * SUB + PB, :] * cw_ref[taps - 1 - k:taps - k, :]
        xrb = xr.astype(BF16)
        rs, igs = [], []
        for hd in range(heads):
            cs = slice(hd * hw, (hd + 1) * hw)
            rs.append(_dot(xrb[:, cs], wa_ref[hd].astype(BF16)) + ba_ref[hd:hd + 1, :])
            igs.append(_dot(xrb[:, cs], wi_ref[hd].astype(BF16)) + bi_ref[hd:hd + 1, :])
        r = jax.nn.sigmoid(jnp.concatenate(rs, axis=1))
        ig = jax.nn.sigmoid(jnp.concatenate(igs, axis=1))
        log_a = (-C_GATE_C * r) * softplus
        a = jnp.exp(log_a)
        th = jnp.tanh(log_a)
        b = jnp.sqrt(-2.0 * th / (1.0 - th)) * (ig * xr)

        acum = a[0:SUB]
        hloc = b[0:SUB]
        abuf[0:SUB, :] = acum
        hbuf[0:SUB, :] = hloc
        for j in range(1, L):
            aj = a[j * SUB:(j + 1) * SUB]
            hloc = aj * hloc + b[j * SUB:(j + 1) * SUB]
            acum = aj * acum
            abuf[j * SUB:(j + 1) * SUB, :] = acum
            hbuf[j * SUB:(j + 1) * SUB, :] = hloc
        for k in (1, 2, 4):
            a_sh = jnp.where(sub < k, 1.0, pltpu.roll(acum, k, 0))
            h_sh = jnp.where(sub < k, 0.0, pltpu.roll(hloc, k, 0))
            hloc = acum * h_sh + hloc
            acum = acum * a_sh
        c0 = hcar[...]
        ends = hloc + acum * c0
        cin = jnp.where(sub == 0, c0, pltpu.roll(ends, 1, 0))
        hcar[...] = jnp.broadcast_to(ends[SUB - 1:SUB, :], (SUB, cwid))
        hs = hbuf[...] + abuf[...] * jnp.concatenate([cin] * L, axis=0)
        o_ref[0, r0:r0 + PB, :] = x + _dot((hs * gate).astype(BF16), wout_ref[...])


def _rglru(x, g, w_in, b_in, conv_w, conv_b, w_a, b_a, w_i, b_i, lam, w_out, tm=512):
    d = x.shape[-1]
    cwid = w_out.shape[0]
    heads = w_a.shape[0]
    taps = conv_w.shape[0]
    consts = [g.reshape(1, d), w_in.astype(BF16), b_in.reshape(1, 2 * cwid), conv_w, conv_b.reshape(1, cwid),
              w_a, b_a, w_i, b_i, lam.reshape(1, cwid), w_out.astype(BF16)]
    body = functools.partial(_rglru_body, nb=tm // PB, cwid=cwid, heads=heads, taps=taps)
    hr = (taps - 1) * SUB
    scratch = [pltpu.VMEM((PB + hr, cwid), F32), pltpu.VMEM((PB, cwid), F32), pltpu.VMEM((PB, cwid), F32),
               pltpu.VMEM((hr, cwid), F32), pltpu.VMEM((SUB, cwid), F32)]
    return _call(body, x, consts, tm, scratch, "mixer_rglru")


def _sconv_body(x_ref, g_ref, win_ref, cw_ref, wout_ref, o_ref, mbuf, tail, *, nb, dw, taps):
    hr = (taps - 1) * SUB

    @pl.when(pl.program_id(1) == 0)
    def _():
        tail[...] = jnp.zeros(tail.shape, F32)

    for n in range(nb):
        r0 = n * PB
        x = x_ref[0, r0:r0 + PB, :]
        h = _rms(x, g_ref[...]).astype(BF16)
        bg = _dot(h, win_ref[:, 0:dw])
        mbuf[hr:hr + PB, :] = _dot(h, win_ref[:, dw:2 * dw]) * _dot(h, win_ref[:, 2 * dw:3 * dw])
        cur = mbuf[PB:PB + hr, :]
        mbuf[0:hr, :] = _halo(cur, tail[...])
        tail[...] = cur
        cv = mbuf[hr:hr + PB, :] * cw_ref[taps - 1:taps, :]
        for k in range(1, taps):
            cv = cv + mbuf[hr - k * SUB:hr - k * SUB + PB, :] * cw_ref[taps - 1 - k:taps - k, :]
        o_ref[0, r0:r0 + PB, :] = x + _dot((bg * cv).astype(BF16), wout_ref[...])


def _sconv(x, g, w_in, conv_w, w_out, tm=512):
    d = x.shape[-1]
    dw = w_out.shape[0]
    taps = conv_w.shape[0]
    consts = [g.reshape(1, d), w_in.astype(BF16), conv_w, w_out.astype(BF16)]
    body = functools.partial(_sconv_body, nb=tm // PB, dw=dw, taps=taps)
    hr = (taps - 1) * SUB
    scratch = [pltpu.VMEM((PB + hr, dw), F32), pltpu.VMEM((hr, dw), F32)]
    return _call(body, x, consts, tm, scratch, "mixer_sconv")


def kernel(x, a_norm_g, a_w_in, a_b_in, a_v_norm_g, a_w_s, a_b_s, a_w_out, b_norm_g, b_w_in, b_w_grp, b_b_grp, b_scale, b_w_out, c_norm_g, c_w_in, c_b_in, c_conv_w, c_conv_b, c_w_a, c_b_a, c_w_i, c_b_i, c_lambda, c_w_out, d_norm_g, d_w_in, d_conv_w, d_w_out, ffn_norm_g, ffn_w_up, ffn_conv_w, ffn_conv_b, ffn_w_down, final_norm_g):
    depth = ffn_norm_g.shape[0]
    n_mixers = 4
    x = _to_perm(x)
    for layer in range(depth):
        m, j = layer % n_mixers, layer // n_mixers
        if m == 0:
            x = _gmlp(x, a_norm_g[j], a_w_in[j], a_b_in[j], a_v_norm_g[j], a_w_s[j], a_b_s[j], a_w_out[j])
        elif m == 1:
            x = _pool(x, b_norm_g[j], b_w_in[j], b_w_grp[j], b_b_grp[j], b_scale[j], b_w_out[j])
        elif m == 2:
            x = _rglru(x, c_norm_g[j], c_w_in[j], c_b_in[j], c_conv_w[j], c_conv_b[j], c_w_a[j], c_b_a[j],
                       c_w_i[j], c_b_i[j], c_lambda[j], c_w_out[j])
        else:
            x = _sconv(x, d_norm_g[j], d_w_in[j], d_conv_w[j], d_w_out[j])
        x = _ffn(x, ffn_norm_g[layer], ffn_w_up[layer], ffn_conv_w[layer], ffn_conv_b[layer],
                 ffn_w_down[layer], final_norm_g, final=(layer == depth - 1))
    return _from_perm(x)
```

```python
import functools

import jax
import jax.numpy as jnp
from jax import lax
from jax.experimental import pallas as pl
from jax.experimental.pallas import tpu as pltpu

EPS = 1e-6
SUB = 8
PB = 256
L = PB // SUB
C_GATE_C = 8.0
B_WINDOWS = (2, 4, 8, 16)
A_CHUNK = 128
VMEM_LIMIT = 56 * 1024 * 1024

F32 = jnp.float32
BF16 = jnp.bfloat16


def _to_perm(x):
    return x.reshape(SUB, L, x.shape[-1]).swapaxes(0, 1).reshape(x.shape)


def _from_perm(x):
    return x.reshape(L, SUB, x.shape[-1]).swapaxes(0, 1).reshape(x.shape)


def _block_time(shape, axis):
    p = lax.broadcasted_iota(jnp.int32, shape, axis)
    return (p & (SUB - 1)) * L + (p >> 3)


def _rms(x, g):
    ms = jnp.mean(x * x, axis=-1, keepdims=True)
    return x * lax.rsqrt(ms + EPS) * g


def _gelu(x):
    c = 0.7978845608028654
    half = 0.5 * x
    return half + half * jnp.tanh(x * (c + (c * 0.044715) * (x * x)))


def _dot(a, b):
    return jnp.dot(a, b, preferred_element_type=F32)


def _roll_tiles(t, k=1):
    parts = [pltpu.roll(t[r:r + SUB], k, 0) for r in range(0, t.shape[0], SUB)]
    return parts[0] if len(parts) == 1 else jnp.concatenate(parts, axis=0)


def _halo(cur_tail, prev_tail):
    first = (lax.broadcasted_iota(jnp.int32, cur_tail.shape, 0) & (SUB - 1)) == 0
    return jnp.where(first, _roll_tiles(prev_tail), _roll_tiles(cur_tail))


def _const_spec(shape):
    nd = len(shape)
    return pl.BlockSpec(shape, lambda b, i: (0,) * nd, pipeline_mode=pl.Buffered(1))


def _tile_spec(tm, d):
    return pl.BlockSpec((1, tm, d), lambda b, i: (b, i, 0))


def _call(body, x, consts, tm, scratch, name):
    bsz, s, d = x.shape
    return pl.pallas_call(
        body,
        grid=(bsz, s // tm),
        in_specs=[_tile_spec(tm, d)] + [_const_spec(c.shape) for c in consts],
        out_specs=_tile_spec(tm, d),
        out_shape=jax.ShapeDtypeStruct(x.shape, x.dtype),
        scratch_shapes=scratch,
        compiler_params=pltpu.CompilerParams(
            dimension_semantics=("arbitrary", "arbitrary"), vmem_limit_bytes=VMEM_LIMIT),
        name=name,
    )(x, *consts)


def _cat2(ref, a, b):
    return jnp.concatenate([ref[:, a], ref[:, b]], axis=1)


def _ffn_body(x_ref, g_ref, wup_ref, cw_ref, cb_ref, wdn_ref, fg_ref, o_ref, zbuf, ybuf, tail, *,
              nb, f, cw, final):
    hr = 2 * SUB

    @pl.when(pl.program_id(1) == 0)
    def _():
        tail[...] = jnp.zeros(tail.shape, F32)

    for n in range(nb):
        r0 = n * PB
        x = x_ref[0, r0:r0 + PB, :]
        h = _rms(x, g_ref[...]).astype(BF16)
        for c in range(f // cw):
            gs = slice(c * cw, (c + 1) * cw)
            vs = slice(f + c * cw, f + (c + 1) * cw)
            zs = slice(2 * c * cw, 2 * (c + 1) * cw)
            zb = zbuf.at[n % 2]
            zb[hr:hr + PB, zs] = _dot(h, _cat2(wup_ref, gs, vs))
            cur = zb[PB:PB + hr, zs]
            zb[0:hr, zs] = _halo(cur, tail[:, zs])
            tail[:, zs] = cur
            w = _cat2(cw_ref, gs, vs)
            zc = (zb[0:PB, zs] * w[0:1] + zb[SUB:SUB + PB, zs] * w[1:2]
                  + zb[hr:hr + PB, zs] * w[2:3] + _cat2(cb_ref, gs, vs))
            gt = zc[:, :cw]
            ybuf[n % 2, :, gs] = ((gt * jax.nn.sigmoid(gt)) * zc[:, cw:]).astype(BF16)
        out = x + _dot(ybuf[n % 2], wdn_ref[...])
        if final:
            out = _from_perm(_rms(out, fg_ref[...]))
        o_ref[0, r0:r0 + PB, :] = out


def _ffn(x, g, w_up, conv_w, conv_b, w_down, final_g, final, tm=512, cw=256):
    d = x.shape[-1]
    f = w_down.shape[0]
    nb = tm // PB
    consts = [g.reshape(1, d), w_up.astype(BF16), conv_w, conv_b.reshape(1, 2 * f),
              w_down.astype(BF16), final_g.reshape(1, d)]
    body = functools.partial(_ffn_body, nb=nb, f=f, cw=cw, final=final)
    scratch = [pltpu.VMEM((2, PB + 2 * SUB, 2 * f), F32), pltpu.VMEM((2, PB, f), BF16),
               pltpu.VMEM((2 * SUB, 2 * f), F32)]
    return _call(body, x, consts, tm, scratch, "ffn_final" if final else "ffn")


def _gmlp_body(x_ref, g_ref, win_ref, bin_ref, vg_ref, ws_ref, bs_ref, wout_ref, o_ref, *,
               nb, aw, groups, first):
    gw = aw // groups
    tp = _block_time((PB, PB), 0)
    tq = _block_time((PB, PB), 1)
    chunk_bits = A_CHUNK.bit_length() - 1
    keep = (tq <= tp) & ((tq >> chunk_bits) == (tp >> chunk_bits))
    for n in range(nb):
        r0 = n * PB
        x = x_ref[0, r0:r0 + PB, :]
        if first:
            x = _to_perm(x)
        h = _rms(x, g_ref[...]).astype(BF16)
        zv = _gelu(_dot(h, win_ref[:, aw:2 * aw]) + bin_ref[:, aw:2 * aw])
        v = _rms(zv, vg_ref[...]).astype(BF16)
        cols = []
        for gi in range(groups):
            ws = jnp.where(keep, ws_ref[gi], 0.0).astype(BF16)
            cols.append(_dot(ws, v[:, gi * gw:(gi + 1) * gw]) + bs_ref[gi])
        vm = jnp.concatenate(cols, axis=1)
        zu = _gelu(_dot(h, win_ref[:, 0:aw]) + bin_ref[:, 0:aw])
        o_ref[0, r0:r0 + PB, :] = x + _dot((zu * vm).astype(BF16), wout_ref[...])


def _gmlp(x, g, w_in, b_in, vg, w_s, b_s, w_out, first, tm=512):
    d = x.shape[-1]
    aw = w_out.shape[0]
    groups = w_s.shape[0]
    nb = tm // PB
    p = jnp.arange(PB)
    idx = ((p % SUB) * L + p // SUB) % A_CHUNK
    ws_p = w_s[:, idx][:, :, idx]
    bs_p = b_s[:, idx][..., None]
    consts = [g.reshape(1, d), w_in.astype(BF16), b_in.reshape(1, 2 * aw), vg.reshape(1, aw),
              ws_p, bs_p, w_out.astype(BF16)]
    body = functools.partial(_gmlp_body, nb=nb, aw=aw, groups=groups, first=first)
    return _call(body, x, consts, tm, [], "mixer_gmlp")


def _pool_body(x_ref, g_ref, win_ref, wg_ref, bg_ref, sc_ref, wout_ref, o_ref, tail, *, nb, bw, groups):
    hr = 16 * SUB
    i = pl.program_id(1)

    @pl.when(i == 0)
    def _():
        tail[...] = jnp.zeros(tail.shape, F32)

    gw = bw // groups
    for n in range(nb):
        r0 = n * PB
        x = x_ref[0, r0:r0 + PB, :]
        h = _rms(x, g_ref[...]).astype(BF16)
        z = _dot(h, win_ref[...])
        cur = z[PB - hr:PB, :]
        ext = jnp.concatenate([_halo(cur, tail[...]), z], axis=0)
        tail[...] = cur
        pos = (i * nb + n) * PB + _block_time((PB, 1), 0)
        ys = []
        for gi, win in enumerate(B_WINDOWS):
            cs = slice(gi * gw, (gi + 1) * gw)
            s = ext[:, cs]
            step = 1
            while step < win:
                s = s[step * SUB:] + s[:-step * SUB]
                step *= 2
            acc = s[s.shape[0] - PB:]
            inv_cnt = 1.0 / jnp.minimum(pos + 1, win).astype(F32)
            p = (acc * inv_cnt - z[:, cs]).astype(BF16)
            ys.append(_dot(p, wg_ref[gi].astype(BF16)) + bg_ref[gi:gi + 1, :])
        y = jnp.concatenate(ys, axis=1) * sc_ref[...]
        o_ref[0, r0:r0 + PB, :] = x + _dot(y.astype(BF16), wout_ref[...])


def _pool(x, g, w_in, w_grp, b_grp, scale, w_out, tm=512):
    d = x.shape[-1]
    bw = w_out.shape[0]
    groups = w_grp.shape[0]
    consts = [g.reshape(1, d), w_in.astype(BF16), w_grp, b_grp, scale.reshape(1, bw), w_out.astype(BF16)]
    body = functools.partial(_pool_body, nb=tm // PB, bw=bw, groups=groups)
    scratch = [pltpu.VMEM((16 * SUB, bw), F32)]
    return _call(body, x, consts, tm, scratch, "mixer_pool")


def _rglru_body(x_ref, g_ref, win_ref, bin_ref, cw_ref, cb_ref, wa_ref, ba_ref, wi_ref, bi_ref, lam_ref,
                wout_ref, o_ref, xbuf, abuf, hbuf, tail, hcar, *, nb, cwid, heads, taps):
    hr = (taps - 1) * SUB

    @pl.when(pl.program_id(1) == 0)
    def _():
        tail[...] = jnp.zeros(tail.shape, F32)
        hcar[...] = jnp.zeros(hcar.shape, F32)

    hw = cwid // heads
    sub = lax.broadcasted_iota(jnp.int32, (SUB, cwid), 0)
    nlam = -lam_ref[...]
    softplus = jnp.maximum(nlam, 0.0) + jnp.log1p(jnp.exp(-jnp.abs(nlam)))
    for n in range(nb):
        r0 = n * PB
        x = x_ref[0, r0:r0 + PB, :]
        h = _rms(x, g_ref[...]).astype(BF16)
        gate = _gelu(_dot(h, win_ref[:, 0:cwid]) + bin_ref[:, 0:cwid])
        xbuf[hr:hr + PB, :] = _dot(h, win_ref[:, cwid:2 * cwid]) + bin_ref[:, cwid:2 * cwid]
        cur = xbuf[PB:PB + hr, :]
        xbuf[0:hr, :] = _halo(cur, tail[...])
        tail[...] = cur
        xr = cb_ref[...] + xbuf[hr:hr + PB, :] * cw_ref[taps - 1:taps, :]
        for k in range(1, taps):
            xr = xr + xbuf[hr - k * SUB:hr - k * SUB + PB, :] * cw_ref[taps - 1 - k:taps - k, :]
        xrb = xr.astype(BF16)
        rs, igs = [], []
        for hd in range(heads):
            w2 = jnp.concatenate([wa_ref[hd], wi_ref[hd]], axis=1).astype(BF16)
            both = _dot(xrb[:, hd * hw:(hd + 1) * hw], w2)
            rs.append(both[:, :hw] + ba_ref[hd:hd + 1, :])
            igs.append(both[:, hw:] + bi_ref[hd:hd + 1, :])
        r = jax.nn.sigmoid(jnp.concatenate(rs, axis=1))
        ig = jax.nn.sigmoid(jnp.concatenate(igs, axis=1))
        log_a = (-C_GATE_C * r) * softplus
        a = jnp.exp(log_a)
        th = jnp.tanh(log_a)
        b = jnp.sqrt(-2.0 * th / (1.0 - th)) * (ig * xr)

        acum = a[0:SUB]
        hloc = b[0:SUB]
        abuf[0:SUB, :] = acum
        hbuf[0:SUB, :] = hloc
        for j in range(1, L):
            aj = a[j * SUB:(j + 1) * SUB]
            hloc = aj * hloc + b[j * SUB:(j + 1) * SUB]
            acum = aj * acum
            abuf[j * SUB:(j + 1) * SUB, :] = acum
            hbuf[j * SUB:(j + 1) * SUB, :] = hloc
        for k in (1, 2, 4):
            a_sh = jnp.where(sub < k, 1.0, pltpu.roll(acum, k, 0))
            h_sh = jnp.where(sub < k, 0.0, pltpu.roll(hloc, k, 0))
            hloc = acum * h_sh + hloc
            acum = acum * a_sh
        c0 = hcar[...]
        ends = hloc + acum * c0
        cin = jnp.where(sub == 0, c0, pltpu.roll(ends, 1, 0))
        hcar[...] = jnp.broadcast_to(ends[SUB - 1:SUB, :], (SUB, cwid))
        hs = hbuf[...] + abuf[...] * jnp.concatenate([cin] * L, axis=0)
        o_ref[0, r0:r0 + PB, :] = x + _dot((hs * gate).astype(BF16), wout_ref[...])


def _rglru(x, g, w_in, b_in, conv_w, conv_b, w_a, b_a, w_i, b_i, lam, w_out, tm=512):
    d = x.shape[-1]
    cwid = w_out.shape[0]
    heads = w_a.shape[0]
    taps = conv_w.shape[0]
    consts = [g.reshape(1, d), w_in.astype(BF16), b_in.reshape(1, 2 * cwid), conv_w, conv_b.reshape(1, cwid),
              w_a, b_a, w_i, b_i, lam.reshape(1, cwid), w_out.astype(BF16)]
    body = functools.partial(_rglru_body, nb=tm // PB, cwid=cwid, heads=heads, taps=taps)
    hr = (taps - 1) * SUB
    scratch = [pltpu.VMEM((PB + hr, cwid), F32), pltpu.VMEM((PB, cwid), F32), pltpu.VMEM((PB, cwid), F32),
               pltpu.VMEM((hr, cwid), F32), pltpu.VMEM((SUB, cwid), F32)]
    return _call(body, x, consts, tm, scratch, "mixer_rglru")


def _sconv_body(x_ref, g_ref, win_ref, cw_ref, wout_ref, o_ref, mbuf, tail, *, nb, dw, taps):
    hr = (taps - 1) * SUB

    @pl.when(pl.program_id(1) == 0)
    def _():
        tail[...] = jnp.zeros(tail.shape, F32)

    for n in range(nb):
        r0 = n * PB
        x = x_ref[0, r0:r0 + PB, :]
        h = _rms(x, g_ref[...]).astype(BF16)
        bg = _dot(h, win_ref[:, 0:dw])
        mbuf[hr:hr + PB, :] = _dot(h, win_ref[:, dw:2 * dw]) * _dot(h, win_ref[:, 2 * dw:3 * dw])
        cur = mbuf[PB:PB + hr, :]
        mbuf[0:hr, :] = _halo(cur, tail[...])
        tail[...] = cur
        cv = mbuf[hr:hr + PB, :] * cw_ref[taps - 1:taps, :]
        for k in range(1, taps):
            cv = cv + mbuf[hr - k * SUB:hr - k * SUB + PB, :] * cw_ref[taps - 1 - k:taps - k, :]
        o_ref[0, r0:r0 + PB, :] = x + _dot((bg * cv).astype(BF16), wout_ref[...])


def _sconv(x, g, w_in, conv_w, w_out, tm=512):
    d = x.shape[-1]
    dw = w_out.shape[0]
    taps = conv_w.shape[0]
    consts = [g.reshape(1, d), w_in.astype(BF16), conv_w, w_out.astype(BF16)]
    body = functools.partial(_sconv_body, nb=tm // PB, dw=dw, taps=taps)
    hr = (taps - 1) * SUB
    scratch = [pltpu.VMEM((PB + hr, dw), F32), pltpu.VMEM((hr, dw), F32)]
    return _call(body, x, consts, tm, scratch, "mixer_sconv")


def kernel(x, a_norm_g, a_w_in, a_b_in, a_v_norm_g, a_w_s, a_b_s, a_w_out, b_norm_g, b_w_in, b_w_grp, b_b_grp, b_scale, b_w_out, c_norm_g, c_w_in, c_b_in, c_conv_w, c_conv_b, c_w_a, c_b_a, c_w_i, c_b_i, c_lambda, c_w_out, d_norm_g, d_w_in, d_conv_w, d_w_out, ffn_norm_g, ffn_w_up, ffn_conv_w, ffn_conv_b, ffn_w_down, final_norm_g):
    depth = ffn_norm_g.shape[0]
    n_mixers = 4
    for layer in range(depth):
        m, j = layer % n_mixers, layer // n_mixers
        if m == 0:
            x = _gmlp(x, a_norm_g[j], a_w_in[j], a_b_in[j], a_v_norm_g[j], a_w_s[j], a_b_s[j], a_w_out[j],
                      first=(layer == 0))
        elif m == 1:
            x = _pool(x, b_norm_g[j], b_w_in[j], b_w_grp[j], b_b_grp[j], b_scale[j], b_w_out[j])
        elif m == 2:
            x = _rglru(x, c_norm_g[j], c_w_in[j], c_b_in[j], c_conv_w[j], c_conv_b[j], c_w_a[j], c_b_a[j],
                       c_w_i[j], c_b_i[j], c_lambda[j], c_w_out[j])
        else:
            x = _sconv(x, d_norm_g[j], d_w_in[j], d_conv_w[j], d_w_out[j])
        x = _ffn(x, ffn_norm_g[layer], ffn_w_up[layer], ffn_conv_w[layer], ffn_conv_b[layer],
                 ffn_w_down[layer], final_norm_g, final=(layer == depth - 1))
    return x
```

```python
import functools

import jax
import jax.numpy as jnp
from jax import lax
from jax.experimental import pallas as pl
from jax.experimental.pallas import tpu as pltpu

EPS = 1e-6
SUB = 8
PB = 256
L = PB // SUB
C_GATE_C = 8.0
B_WINDOWS = (2, 4, 8, 16)
A_CHUNK = 128
FFN_CHUNK = 256
FFN_SLOTS = 4
VMEM_LIMIT = 56 * 1024 * 1024

F32 = jnp.float32
BF16 = jnp.bfloat16


def _to_perm(x):
    return x.reshape(SUB, L, x.shape[-1]).swapaxes(0, 1).reshape(x.shape)


def _from_perm(x):
    return x.reshape(L, SUB, x.shape[-1]).swapaxes(0, 1).reshape(x.shape)


def _block_time(shape, axis):
    p = lax.broadcasted_iota(jnp.int32, shape, axis)
    return (p & (SUB - 1)) * L + (p >> 3)


def _rms(x, g):
    ms = jnp.mean(x * x, axis=-1, keepdims=True)
    return x * lax.rsqrt(ms + EPS) * g


def _gelu(x):
    c = 0.7978845608028654
    half = 0.5 * x
    return half + half * jnp.tanh(x * (c + (c * 0.044715) * (x * x)))


def _dot(a, b):
    return jnp.dot(a, b, preferred_element_type=F32)


def _roll_tiles(t, k=1):
    parts = [pltpu.roll(t[r:r + SUB], k, 0) for r in range(0, t.shape[0], SUB)]
    return parts[0] if len(parts) == 1 else jnp.concatenate(parts, axis=0)


def _halo(cur_tail, prev_tail):
    first = (lax.broadcasted_iota(jnp.int32, cur_tail.shape, 0) & (SUB - 1)) == 0
    return jnp.where(first, _roll_tiles(prev_tail), _roll_tiles(cur_tail))


def _cat2(ref, a, b):
    return jnp.concatenate([ref[:, a], ref[:, b]], axis=1)


def _zero(*refs):
    for r in refs:
        r[...] = jnp.zeros(r.shape, r.dtype)


def _ffn_consts(g, w_up, conv_w, conv_b, w_down):
    d, f = w_down.shape[1], w_down.shape[0]
    return [g.reshape(1, d), w_up.astype(BF16), conv_w, conv_b.reshape(1, 2 * f), w_down.astype(BF16)]


def _ffn_scratch(f):
    hr = 2 * SUB
    return [pltpu.VMEM((FFN_SLOTS, PB + hr, 2 * FFN_CHUNK), F32), pltpu.VMEM((2, PB, f), BF16),
            pltpu.VMEM((hr, 2 * f), F32)]


def _ffn_block(x, n, blk, consts, scratch):
    g_ref, wup_ref, cw_ref, cb_ref, wdn_ref = consts
    zbuf, ybuf, tail = scratch
    f, cw, hr = wdn_ref.shape[0], FFN_CHUNK, 2 * SUB
    h = _rms(x, g_ref[...]).astype(BF16)
    for c in range(f // cw):
        gs = slice(c * cw, (c + 1) * cw)
        vs = slice(f + c * cw, f + (c + 1) * cw)
        zs = slice(2 * c * cw, 2 * (c + 1) * cw)
        zb = zbuf.at[(n * (f // cw) + c) % FFN_SLOTS]
        zb[hr:hr + PB, :] = _dot(h, _cat2(wup_ref, gs, vs))
        cur = zb[PB:PB + hr, :]
        zb[0:hr, :] = _halo(cur, tail[:, zs])
        tail[:, zs] = cur
        w = _cat2(cw_ref, gs, vs)
        zc = (zb[0:PB, :] * w[0:1] + zb[SUB:SUB + PB, :] * w[1:2]
              + zb[hr:hr + PB, :] * w[2:3] + _cat2(cb_ref, gs, vs))
        gt = zc[:, :cw]
        ybuf[n % 2, :, gs] = ((gt * jax.nn.sigmoid(gt)) * zc[:, cw:]).astype(BF16)
    return x + _dot(ybuf[n % 2], wdn_ref[...])


def _gmlp_consts(g, w_in, b_in, vg, w_s, b_s, w_out):
    d, aw = w_out.shape[1], w_out.shape[0]
    p = jnp.arange(PB)
    idx = ((p % SUB) * L + p // SUB) % A_CHUNK
    return [g.reshape(1, d), w_in.astype(BF16), b_in.reshape(1, 2 * aw), vg.reshape(1, aw),
            w_s[:, idx][:, :, idx], b_s[:, idx][..., None], w_out.astype(BF16)]


def _gmlp_scratch(aw):
    return []


def _gmlp_block(x, n, blk, consts, scratch):
    g_ref, win_ref, bin_ref, vg_ref, ws_ref, bs_ref, wout_ref = consts
    aw, groups = wout_ref.shape[0], ws_ref.shape[0]
    gw = aw // groups
    tp = _block_time((PB, PB), 0)
    tq = _block_time((PB, PB), 1)
    chunk_bits = A_CHUNK.bit_length() - 1
    keep = (tq <= tp) & ((tq >> chunk_bits) == (tp >> chunk_bits))
    h = _rms(x, g_ref[...]).astype(BF16)
    zv = _gelu(_dot(h, win_ref[:, aw:2 * aw]) + bin_ref[:, aw:2 * aw])
    v = _rms(zv, vg_ref[...]).astype(BF16)
    cols = []
    for gi in range(groups):
        ws = jnp.where(keep, ws_ref[gi], 0.0).astype(BF16)
        cols.append(_dot(ws, v[:, gi * gw:(gi + 1) * gw]) + bs_ref[gi])
    vm = jnp.concatenate(cols, axis=1)
    zu = _gelu(_dot(h, win_ref[:, 0:aw]) + bin_ref[:, 0:aw])
    return x + _dot((zu * vm).astype(BF16), wout_ref[...])


POOL_HALO = 16 * SUB


def _pool_consts(g, w_in, w_grp, b_grp, scale, w_out):
    d, bw = w_out.shape[1], w_out.shape[0]
    return [g.reshape(1, d), w_in.astype(BF16), w_grp, b_grp, scale.reshape(1, bw), w_out.astype(BF16)]


def _pool_scratch(bw):
    return [pltpu.VMEM((POOL_HALO, bw), F32)]


def _pool_block(x, n, blk, consts, scratch):
    g_ref, win_ref, wg_ref, bg_ref, sc_ref, wout_ref = consts
    (tail,) = scratch
    bw, groups = wout_ref.shape[0], wg_ref.shape[0]
    gw = bw // groups
    h = _rms(x, g_ref[...]).astype(BF16)
    z = _dot(h, win_ref[...])
    cur = z[PB - POOL_HALO:PB, :]
    ext = jnp.concatenate([_halo(cur, tail[...]), z], axis=0)
    tail[...] = cur
    pos = blk * PB + _block_time((PB, 1), 0)
    ys = []
    for gi, win in enumerate(B_WINDOWS):
        cs = slice(gi * gw, (gi + 1) * gw)
        s = ext[:, cs]
        step = 1
        while step < win:
            s = s[step * SUB:] + s[:-step * SUB]
            step *= 2
        acc = s[s.shape[0] - PB:]
        inv_cnt = 1.0 / jnp.minimum(pos + 1, win).astype(F32)
        p = (acc * inv_cnt - z[:, cs]).astype(BF16)
        ys.append(_dot(p, wg_ref[gi].astype(BF16)) + bg_ref[gi:gi + 1, :])
    y = jnp.concatenate(ys, axis=1) * sc_ref[...]
    return x + _dot(y.astype(BF16), wout_ref[...])


def _rglru_consts(g, w_in, b_in, conv_w, conv_b, w_a, b_a, w_i, b_i, lam, w_out):
    d, cwid = w_out.shape[1], w_out.shape[0]
    return [g.reshape(1, d), w_in.astype(BF16), b_in.reshape(1, 2 * cwid), conv_w, conv_b.reshape(1, cwid),
            w_a, b_a, w_i, b_i, lam.reshape(1, cwid), w_out.astype(BF16)]


def _rglru_scratch(cwid, taps):
    hr = (taps - 1) * SUB
    return [pltpu.VMEM((PB + hr, cwid), F32), pltpu.VMEM((PB, cwid), F32), pltpu.VMEM((PB, cwid), F32),
            pltpu.VMEM((hr, cwid), F32), pltpu.VMEM((SUB, cwid), F32)]


def _rglru_block(x, n, blk, consts, scratch):
    g_ref, win_ref, bin_ref, cw_ref, cb_ref, wa_ref, ba_ref, wi_ref, bi_ref, lam_ref, wout_ref = consts
    xbuf, abuf, hbuf, tail, hcar = scratch
    cwid, heads, taps = wout_ref.shape[0], wa_ref.shape[0], cw_ref.shape[0]
    hr = (taps - 1) * SUB
    hw = cwid // heads
    sub = lax.broadcasted_iota(jnp.int32, (SUB, cwid), 0)
    nlam = -lam_ref[...]
    softplus = jnp.maximum(nlam, 0.0) + jnp.log1p(jnp.exp(-jnp.abs(nlam)))
    h = _rms(x, g_ref[...]).astype(BF16)
    gate = _gelu(_dot(h, win_ref[:, 0:cwid]) + bin_ref[:, 0:cwid])
    xbuf[hr:hr + PB, :] = _dot(h, win_ref[:, cwid:2 * cwid]) + bin_ref[:, cwid:2 * cwid]
    cur = xbuf[PB:PB + hr, :]
    xbuf[0:hr, :] = _halo(cur, tail[...])
    tail[...] = cur
    xr = cb_ref[...] + xbuf[hr:hr + PB, :] * cw_ref[taps - 1:taps, :]
    for k in range(1, taps):
        xr = xr + xbuf[hr - k * SUB:hr - k * SUB + PB, :] * cw_ref[taps - 1 - k:taps - k, :]
    xrb = xr.astype(BF16)
    rs, igs = [], []
    for hd in range(heads):
        w2 = jnp.concatenate([wa_ref[hd], wi_ref[hd]], axis=1).astype(BF16)
        both = _dot(xrb[:, hd * hw:(hd + 1) * hw], w2)
        rs.append(both[:, :hw] + ba_ref[hd:hd + 1, :])
        igs.append(both[:, hw:] + bi_ref[hd:hd + 1, :])
    r = jax.nn.sigmoid(jnp.concatenate(rs, axis=1))
    ig = jax.nn.sigmoid(jnp.concatenate(igs, axis=1))
    log_a = (-C_GATE_C * r) * softplus
    a = jnp.exp(log_a)
    th = jnp.tanh(log_a)
    b = jnp.sqrt(-2.0 * th / (1.0 - th)) * (ig * xr)

    acum = a[0:SUB]
    hloc = b[0:SUB]
    abuf[0:SUB, :] = acum
    hbuf[0:SUB, :] = hloc
    for j in range(1, L):
        aj = a[j * SUB:(j + 1) * SUB]
        hloc = aj * hloc + b[j * SUB:(j + 1) * SUB]
        acum = aj * acum
        abuf[j * SUB:(j + 1) * SUB, :] = acum
        hbuf[j * SUB:(j + 1) * SUB, :] = hloc
    for k in (1, 2, 4):
        a_sh = jnp.where(sub < k, 1.0, pltpu.roll(acum, k, 0))
        h_sh = jnp.where(sub < k, 0.0, pltpu.roll(hloc, k, 0))
        hloc = acum * h_sh + hloc
        acum = acum * a_sh
    c0 = hcar[...]
    ends = hloc + acum * c0
    cin = jnp.where(sub == 0, c0, pltpu.roll(ends, 1, 0))
    hcar[...] = jnp.broadcast_to(ends[SUB - 1:SUB, :], (SUB, cwid))
    hs = hbuf[...] + abuf[...] * jnp.concatenate([cin] * L, axis=0)
    return x + _dot((hs * gate).astype(BF16), wout_ref[...])


def _sconv_consts(g, w_in, conv_w, w_out):
    d = w_out.shape[1]
    return [g.reshape(1, d), w_in.astype(BF16), conv_w, w_out.astype(BF16)]


def _sconv_scratch(dw, taps):
    hr = (taps - 1) * SUB
    return [pltpu.VMEM((PB + hr, dw), F32), pltpu.VMEM((hr, dw), F32)]


def _sconv_block(x, n, blk, consts, scratch):
    g_ref, win_ref, cw_ref, wout_ref = consts
    mbuf, tail = scratch
    dw, taps = wout_ref.shape[0], cw_ref.shape[0]
    hr = (taps - 1) * SUB
    h = _rms(x, g_ref[...]).astype(BF16)
    bg = _dot(h, win_ref[:, 0:dw])
    mbuf[hr:hr + PB, :] = _dot(h, win_ref[:, dw:2 * dw]) * _dot(h, win_ref[:, 2 * dw:3 * dw])
    cur = mbuf[PB:PB + hr, :]
    mbuf[0:hr, :] = _halo(cur, tail[...])
    tail[...] = cur
    cv = mbuf[hr:hr + PB, :] * cw_ref[taps - 1:taps, :]
    for k in range(1, taps):
        cv = cv + mbuf[hr - k * SUB:hr - k * SUB + PB, :] * cw_ref[taps - 1 - k:taps - k, :]
    return x + _dot((bg * cv).astype(BF16), wout_ref[...])


def _layer_body(*refs, mixer_block, n_mix, n_mix_scr, nb, first, last):
    x_ref = refs[0]
    mix_consts = refs[1:1 + n_mix]
    ffn_consts = refs[1 + n_mix:1 + n_mix + 5]
    fg_ref = refs[1 + n_mix + 5]
    o_ref = refs[1 + n_mix + 6]
    scratch = refs[1 + n_mix + 7:]
    mix_scr, ffn_scr = scratch[:n_mix_scr], scratch[n_mix_scr:]
    i = pl.program_id(1)

    @pl.when(i == 0)
    def _():
        _zero(*mix_scr[len(mix_scr) - _N_CARRY[mixer_block]:], ffn_scr[-1])

    for n in range(nb):
        r0 = n * PB
        x = x_ref[0, r0:r0 + PB, :]
        if first:
            x = _to_perm(x)
        x = mixer_block(x, n, i * nb + n, mix_consts, mix_scr)
        x = _ffn_block(x, n, i * nb + n, ffn_consts, ffn_scr)
        if last:
            x = _from_perm(_rms(x, fg_ref[...]))
        o_ref[0, r0:r0 + PB, :] = x


_N_CARRY = {_gmlp_block: 0, _pool_block: 1, _rglru_block: 2, _sconv_block: 1}


def _const_spec(shape):
    nd = len(shape)
    return pl.BlockSpec(shape, lambda b, i: (0,) * nd, pipeline_mode=pl.Buffered(1))


def _layer(x, mixer_block, mix_consts, mix_scratch, ffn_consts, final_g, first, last, name, tm=512):
    bsz, s, d = x.shape
    f = ffn_consts[-1].shape[0]
    consts = list(mix_consts) + list(ffn_consts) + [final_g.reshape(1, d)]
    body = functools.partial(_layer_body, mixer_block=mixer_block, n_mix=len(mix_consts),
                             n_mix_scr=len(mix_scratch), nb=tm // PB, first=first, last=last)
    tile = pl.BlockSpec((1, tm, d), lambda b, i: (b, i, 0))
    return pl.pallas_call(
        body,
        grid=(bsz, s // tm),
        in_specs=[tile] + [_const_spec(c.shape) for c in consts],
        out_specs=tile,
        out_shape=jax.ShapeDtypeStruct(x.shape, x.dtype),
        scratch_shapes=list(mix_scratch) + _ffn_scratch(f),
        compiler_params=pltpu.CompilerParams(
            dimension_semantics=("arbitrary", "arbitrary"), vmem_limit_bytes=VMEM_LIMIT),
        name=name,
    )(x, *consts)


def kernel(x, a_norm_g, a_w_in, a_b_in, a_v_norm_g, a_w_s, a_b_s, a_w_out, b_norm_g, b_w_in, b_w_grp, b_b_grp, b_scale, b_w_out, c_norm_g, c_w_in, c_b_in, c_conv_w, c_conv_b, c_w_a, c_b_a, c_w_i, c_b_i, c_lambda, c_w_out, d_norm_g, d_w_in, d_conv_w, d_w_out, ffn_norm_g, ffn_w_up, ffn_conv_w, ffn_conv_b, ffn_w_down, final_norm_g):
    depth = ffn_norm_g.shape[0]
    n_mixers = 4
    for layer in range(depth):
        m, j = layer % n_mixers, layer // n_mixers
        if m == 0:
            block, name = _gmlp_block, "layer_gmlp"
            mc = _gmlp_consts(a_norm_g[j], a_w_in[j], a_b_in[j], a_v_norm_g[j], a_w_s[j], a_b_s[j], a_w_out[j])
            ms = _gmlp_scratch(a_w_out.shape[1])
        elif m == 1:
            block, name = _pool_block, "layer_pool"
            mc = _pool_consts(b_norm_g[j], b_w_in[j], b_w_grp[j], b_b_grp[j], b_scale[j], b_w_out[j])
            ms = _pool_scratch(b_w_out.shape[1])
        elif m == 2:
            block, name = _rglru_block, "layer_rglru"
            mc = _rglru_consts(c_norm_g[j], c_w_in[j], c_b_in[j], c_conv_w[j], c_conv_b[j], c_w_a[j], c_b_a[j],
                               c_w_i[j], c_b_i[j], c_lambda[j], c_w_out[j])
            ms = _rglru_scratch(c_w_out.shape[1], c_conv_w.shape[1])
        else:
            block, name = _sconv_block, "layer_sconv"
            mc = _sconv_consts(d_norm_g[j], d_w_in[j], d_conv_w[j], d_w_out[j])
            ms = _sconv_scratch(d_w_out.shape[1], d_conv_w.shape[1])
        fc = _ffn_consts(ffn_norm_g[layer], ffn_w_up[layer], ffn_conv_w[layer], ffn_conv_b[layer],
                         ffn_w_down[layer])
        x = _layer(x, block, mc, ms, fc, final_norm_g, first=(layer == 0), last=(layer == depth - 1), name=name)
    return x
```

```python
import functools

import jax
import jax.numpy as jnp
from jax import lax
from jax.experimental import pallas as pl
from jax.experimental.pallas import tpu as pltpu

EPS = 1e-6
SUB = 8
PB = 256
L = PB // SUB
C_GATE_C = 8.0
B_WINDOWS = (2, 4, 8, 16)
A_CHUNK = 128
FFN_CHUNK = 256
FFN_SLOTS = 4
VMEM_LIMIT = 56 * 1024 * 1024

F32 = jnp.float32
BF16 = jnp.bfloat16


def _to_perm(x):
    return x.reshape(SUB, L, x.shape[-1]).swapaxes(0, 1).reshape(x.shape)


def _from_perm(x):
    return x.reshape(L, SUB, x.shape[-1]).swapaxes(0, 1).reshape(x.shape)


def _block_time(shape, axis):
    p = lax.broadcasted_iota(jnp.int32, shape, axis)
    return (p & (SUB - 1)) * L + (p >> 3)


def _rms(x, g):
    ms = jnp.mean(x * x, axis=-1, keepdims=True)
    return x * lax.rsqrt(ms + EPS) * g


def _gelu(x):
    c = 0.7978845608028654
    half = 0.5 * x
    return half + half * jnp.tanh(x * (c + (c * 0.044715) * (x * x)))


def _dot(a, b):
    return jnp.dot(a, b, preferred_element_type=F32)


def _roll_tiles(t, k=1):
    parts = [pltpu.roll(t[r:r + SUB], k, 0) for r in range(0, t.shape[0], SUB)]
    return parts[0] if len(parts) == 1 else jnp.concatenate(parts, axis=0)


def _halo(cur_tail, prev_tail):
    first = (lax.broadcasted_iota(jnp.int32, cur_tail.shape, 0) & (SUB - 1)) == 0
    return jnp.where(first, _roll_tiles(prev_tail), _roll_tiles(cur_tail))


def _cat2(ref, a, b):
    return jnp.concatenate([ref[:, a], ref[:, b]], axis=1)


def _zero(*refs):
    for r in refs:
        r[...] = jnp.zeros(r.shape, r.dtype)


def _ffn_consts(g, w_up, conv_w, conv_b, w_down):
    d, f = w_down.shape[1], w_down.shape[0]
    return [g.reshape(1, d), w_up.astype(BF16), conv_w, conv_b.reshape(1, 2 * f), w_down.astype(BF16)]


def _ffn_scratch(f):
    hr = 2 * SUB
    return [pltpu.VMEM((FFN_SLOTS, PB + hr, 2 * FFN_CHUNK), F32), pltpu.VMEM((2, PB, f), BF16),
            pltpu.VMEM((hr, 2 * f), F32)]


def _ffn_block(x, n, blk, consts, scratch):
    g_ref, wup_ref, cw_ref, cb_ref, wdn_ref = consts
    zbuf, ybuf, tail = scratch
    f, cw, hr = wdn_ref.shape[0], FFN_CHUNK, 2 * SUB
    h = _rms(x, g_ref[...]).astype(BF16)
    for c in range(f // cw):
        gs = slice(c * cw, (c + 1) * cw)
        vs = slice(f + c * cw, f + (c + 1) * cw)
        zs = slice(2 * c * cw, 2 * (c + 1) * cw)
        zb = zbuf.at[(n * (f // cw) + c) % FFN_SLOTS]
        zb[hr:hr + PB, :] = _dot(h, _cat2(wup_ref, gs, vs))
        cur = zb[PB:PB + hr, :]
        zb[0:hr, :] = _halo(cur, tail[:, zs])
        tail[:, zs] = cur
        w = _cat2(cw_ref, gs, vs)
        zc = (zb[0:PB, :] * w[0:1] + zb[SUB:SUB + PB, :] * w[1:2]
              + zb[hr:hr + PB, :] * w[2:3] + _cat2(cb_ref, gs, vs))
        gt = zc[:, :cw]
        ybuf[n % 2, :, gs] = ((gt * jax.nn.sigmoid(gt)) * zc[:, cw:]).astype(BF16)
        yield
    return x + _dot(ybuf[n % 2], wdn_ref[...])


def _gmlp_consts(g, w_in, b_in, vg, w_s, b_s, w_out):
    d, aw = w_out.shape[1], w_out.shape[0]
    p = jnp.arange(PB)
    idx = ((p % SUB) * L + p // SUB) % A_CHUNK
    return [g.reshape(1, d), w_in.astype(BF16), b_in.reshape(1, 2 * aw), vg.reshape(1, aw),
            w_s[:, idx][:, :, idx], b_s[:, idx][..., None], w_out.astype(BF16)]


def _gmlp_scratch(aw):
    return []


def _gmlp_block(x, n, blk, consts, scratch):
    g_ref, win_ref, bin_ref, vg_ref, ws_ref, bs_ref, wout_ref = consts
    aw, groups = wout_ref.shape[0], ws_ref.shape[0]
    gw = aw // groups
    tp = _block_time((PB, PB), 0)
    tq = _block_time((PB, PB), 1)
    chunk_bits = A_CHUNK.bit_length() - 1
    keep = (tq <= tp) & ((tq >> chunk_bits) == (tp >> chunk_bits))
    h = _rms(x, g_ref[...]).astype(BF16)
    zvs = []
    for gi in range(groups):
        vs = slice(aw + gi * gw, aw + (gi + 1) * gw)
        zvs.append(_gelu(_dot(h, win_ref[:, vs]) + bin_ref[:, vs]))
        yield
    v = _rms(jnp.concatenate(zvs, axis=1), vg_ref[...]).astype(BF16)
    yield
    ys = []
    for gi in range(groups):
        cs = slice(gi * gw, (gi + 1) * gw)
        ws = jnp.where(keep, ws_ref[gi], 0.0).astype(BF16)
        vm = _dot(ws, v[:, cs]) + bs_ref[gi]
        zu = _gelu(_dot(h, win_ref[:, cs]) + bin_ref[:, cs])
        ys.append((zu * vm).astype(BF16))
        yield
    return x + _dot(jnp.concatenate(ys, axis=1), wout_ref[...])


POOL_HALO = 16 * SUB


def _pool_consts(g, w_in, w_grp, b_grp, scale, w_out):
    d, bw = w_out.shape[1], w_out.shape[0]
    return [g.reshape(1, d), w_in.astype(BF16), w_grp, b_grp, scale.reshape(1, bw), w_out.astype(BF16)]


def _pool_scratch(bw):
    return [pltpu.VMEM((POOL_HALO, bw), F32)]


def _pool_block(x, n, blk, consts, scratch):
    g_ref, win_ref, wg_ref, bg_ref, sc_ref, wout_ref = consts
    (tail,) = scratch
    bw, groups = wout_ref.shape[0], wg_ref.shape[0]
    gw = bw // groups
    h = _rms(x, g_ref[...]).astype(BF16)
    z = _dot(h, win_ref[...])
    yield
    cur = z[PB - POOL_HALO:PB, :]
    ext = jnp.concatenate([_halo(cur, tail[...]), z], axis=0)
    tail[...] = cur
    pos = blk * PB + _block_time((PB, 1), 0)
    ys = []
    for gi, win in enumerate(B_WINDOWS):
        yield
        cs = slice(gi * gw, (gi + 1) * gw)
        s = ext[:, cs]
        step = 1
        while step < win:
            s = s[step * SUB:] + s[:-step * SUB]
            step *= 2
        acc = s[s.shape[0] - PB:]
        inv_cnt = 1.0 / jnp.minimum(pos + 1, win).astype(F32)
        p = (acc * inv_cnt - z[:, cs]).astype(BF16)
        ys.append(_dot(p, wg_ref[gi].astype(BF16)) + bg_ref[gi:gi + 1, :])
    yield
    y = jnp.concatenate(ys, axis=1) * sc_ref[...]
    return x + _dot(y.astype(BF16), wout_ref[...])


def _rglru_consts(g, w_in, b_in, conv_w, conv_b, w_a, b_a, w_i, b_i, lam, w_out):
    d, cwid = w_out.shape[1], w_out.shape[0]
    return [g.reshape(1, d), w_in.astype(BF16), b_in.reshape(1, 2 * cwid), conv_w, conv_b.reshape(1, cwid),
            w_a, b_a, w_i, b_i, lam.reshape(1, cwid), w_out.astype(BF16)]


def _rglru_scratch(cwid, taps):
    hr = (taps - 1) * SUB
    return [pltpu.VMEM((PB + hr, cwid), F32), pltpu.VMEM((PB, cwid), F32), pltpu.VMEM((PB, cwid), F32),
            pltpu.VMEM((hr, cwid), F32), pltpu.VMEM((SUB, cwid), F32)]


def _rglru_block(x, n, blk, consts, scratch):
    g_ref, win_ref, bin_ref, cw_ref, cb_ref, wa_ref, ba_ref, wi_ref, bi_ref, lam_ref, wout_ref = consts
    xbuf, abuf, hbuf, tail, hcar = scratch
    cwid, heads, taps = wout_ref.shape[0], wa_ref.shape[0], cw_ref.shape[0]
    hr = (taps - 1) * SUB
    hw = cwid // heads
    sub = lax.broadcasted_iota(jnp.int32, (SUB, cwid), 0)
    nlam = -lam_ref[...]
    softplus = jnp.maximum(nlam, 0.0) + jnp.log1p(jnp.exp(-jnp.abs(nlam)))
    h = _rms(x, g_ref[...]).astype(BF16)
    yield
    xbuf[hr:hr + PB, :] = _dot(h, win_ref[:, cwid:2 * cwid]) + bin_ref[:, cwid:2 * cwid]
    cur = xbuf[PB:PB + hr, :]
    xbuf[0:hr, :] = _halo(cur, tail[...])
    tail[...] = cur
    yield
    xr = cb_ref[...] + xbuf[hr:hr + PB, :] * cw_ref[taps - 1:taps, :]
    for k in range(1, taps):
        xr = xr + xbuf[hr - k * SUB:hr - k * SUB + PB, :] * cw_ref[taps - 1 - k:taps - k, :]
    xrb = xr.astype(BF16)
    yield
    a_cols, b_cols = [], []
    for hd in range(heads):
        cs = slice(hd * hw, (hd + 1) * hw)
        w2 = jnp.concatenate([wa_ref[hd], wi_ref[hd]], axis=1).astype(BF16)
        both = _dot(xrb[:, cs], w2)
        r = jax.nn.sigmoid(both[:, :hw] + ba_ref[hd:hd + 1, :])
        ig = jax.nn.sigmoid(both[:, hw:] + bi_ref[hd:hd + 1, :])
        log_a = (-C_GATE_C * r) * softplus[:, cs]
        a_cols.append(jnp.exp(log_a))
        th = jnp.tanh(log_a)
        b_cols.append(jnp.sqrt(-2.0 * th / (1.0 - th)) * (ig * xr[:, cs]))
        yield
    a = jnp.concatenate(a_cols, axis=1)
    b = jnp.concatenate(b_cols, axis=1)

    acum = a[0:SUB]
    hloc = b[0:SUB]
    abuf[0:SUB, :] = acum
    hbuf[0:SUB, :] = hloc
    for j in range(1, L):
        aj = a[j * SUB:(j + 1) * SUB]
        hloc = aj * hloc + b[j * SUB:(j + 1) * SUB]
        acum = aj * acum
        abuf[j * SUB:(j + 1) * SUB, :] = acum
        hbuf[j * SUB:(j + 1) * SUB, :] = hloc
        if j % SUB == SUB - 1:
            yield
    for k in (1, 2, 4):
        a_sh = jnp.where(sub < k, 1.0, pltpu.roll(acum, k, 0))
        h_sh = jnp.where(sub < k, 0.0, pltpu.roll(hloc, k, 0))
        hloc = acum * h_sh + hloc
        acum = acum * a_sh
    c0 = hcar[...]
    ends = hloc + acum * c0
    cin = jnp.where(sub == 0, c0, pltpu.roll(ends, 1, 0))
    hcar[...] = jnp.broadcast_to(ends[SUB - 1:SUB, :], (SUB, cwid))
    hs = hbuf[...] + abuf[...] * jnp.concatenate([cin] * L, axis=0)
    yield
    gate = _gelu(_dot(h, win_ref[:, 0:cwid]) + bin_ref[:, 0:cwid])
    return x + _dot((hs * gate).astype(BF16), wout_ref[...])


def _sconv_consts(g, w_in, conv_w, w_out):
    d = w_out.shape[1]
    return [g.reshape(1, d), w_in.astype(BF16), conv_w, w_out.astype(BF16)]


def _sconv_scratch(dw, taps):
    hr = (taps - 1) * SUB
    return [pltpu.VMEM((PB + hr, dw), F32), pltpu.VMEM((hr, dw), F32)]


def _sconv_block(x, n, blk, consts, scratch):
    g_ref, win_ref, cw_ref, wout_ref = consts
    mbuf, tail = scratch
    dw, taps = wout_ref.shape[0], cw_ref.shape[0]
    hr = (taps - 1) * SUB
    h = _rms(x, g_ref[...]).astype(BF16)
    yield
    cg = _dot(h, win_ref[:, dw:2 * dw])
    yield
    mbuf[hr:hr + PB, :] = cg * _dot(h, win_ref[:, 2 * dw:3 * dw])
    cur = mbuf[PB:PB + hr, :]
    mbuf[0:hr, :] = _halo(cur, tail[...])
    tail[...] = cur
    yield
    cv = mbuf[hr:hr + PB, :] * cw_ref[taps - 1:taps, :]
    for k in range(1, taps):
        cv = cv + mbuf[hr - k * SUB:hr - k * SUB + PB, :] * cw_ref[taps - 1 - k:taps - k, :]
    yield
    y = (_dot(h, win_ref[:, 0:dw]) * cv).astype(BF16)
    yield
    return x + _dot(y, wout_ref[...])


def _trace_alternately(a, b):
    gens, results = [a, b], [None, None]
    live = [True, True]
    while any(live):
        for k in range(2):
            if live[k]:
                try:
                    next(gens[k])
                except StopIteration as done:
                    results[k], live[k] = done.value, False
    return results
def _layer_body(*refs, mixer_block, n_mix, n_mix_scr, nb, nt, first, last):
    x_ref = refs[0]
    mix_consts = refs[1:1 + n_mix]
    ffn_consts = refs[1 + n_mix:1 + n_mix + 5]
    fg_ref = refs[1 + n_mix + 5]
    o_ref = refs[1 + n_mix + 6]
    scratch = refs[1 + n_mix + 7:]
    mid, mix_scr, ffn_scr = scratch[0], scratch[1:1 + n_mix_scr], scratch[1 + n_mix_scr:]
    t = pl.program_id(0)
    mix_tile = lax.rem(t, nt)
    ffn_tile = lax.rem(t + nt - 1, nt)

    @pl.when(t == 0)
    def _():
        _zero(mid)

    @pl.when(mix_tile == 0)
    def _():
        _zero(*mix_scr[len(mix_scr) - _N_CARRY[mixer_block]:])

    @pl.when((ffn_tile == 0) | (t == 0))
    def _():
        _zero(ffn_scr[-1])

    for n in range(nb):
        rows = slice(n * PB, (n + 1) * PB)
        x = x_ref[0, rows, :]
        if first:
            x = _to_perm(x)
        y, m = _trace_alternately(_ffn_block(mid[rows, :], n, ffn_tile * nb + n, ffn_consts, ffn_scr),
                                  mixer_block(x, n, mix_tile * nb + n, mix_consts, mix_scr))
        if last:
            y = _from_perm(_rms(y, fg_ref[...]))
        o_ref[0, rows, :] = y
        mid[rows, :] = m


_N_CARRY = {_gmlp_block: 0, _pool_block: 1, _rglru_block: 2, _sconv_block: 1}


def _const_spec(shape):
    nd = len(shape)
    return pl.BlockSpec(shape, lambda t: (0,) * nd, pipeline_mode=pl.Buffered(1))


def _layer(x, mixer_block, mix_consts, mix_scratch, ffn_consts, final_g, first, last, name, tm=512):
    bsz, s, d = x.shape
    f = ffn_consts[-1].shape[0]
    nt = s // tm
    tiles = bsz * nt
    consts = list(mix_consts) + list(ffn_consts) + [final_g.reshape(1, d)]
    body = functools.partial(_layer_body, mixer_block=mixer_block, n_mix=len(mix_consts),
                             n_mix_scr=len(mix_scratch), nb=tm // PB, nt=nt, first=first, last=last)

    def in_map(t):
        tt = jnp.minimum(t, tiles - 1)
        return (lax.div(tt, nt), lax.rem(tt, nt), 0)

    def out_map(t):
        tt = jnp.maximum(t - 1, 0)
        return (lax.div(tt, nt), lax.rem(tt, nt), 0)

    return pl.pallas_call(
        body,
        grid=(tiles + 1,),
        in_specs=[pl.BlockSpec((1, tm, d), in_map)] + [_const_spec(c.shape) for c in consts],
        out_specs=pl.BlockSpec((1, tm, d), out_map),
        out_shape=jax.ShapeDtypeStruct(x.shape, x.dtype),
        scratch_shapes=[pltpu.VMEM((tm, d), F32)] + list(mix_scratch) + _ffn_scratch(f),
        compiler_params=pltpu.CompilerParams(
            dimension_semantics=("arbitrary",), vmem_limit_bytes=VMEM_LIMIT),
        name=name,
    )(x, *consts)


def kernel(x, a_norm_g, a_w_in, a_b_in, a_v_norm_g, a_w_s, a_b_s, a_w_out, b_norm_g, b_w_in, b_w_grp, b_b_grp, b_scale, b_w_out, c_norm_g, c_w_in, c_b_in, c_conv_w, c_conv_b, c_w_a, c_b_a, c_w_i, c_b_i, c_lambda, c_w_out, d_norm_g, d_w_in, d_conv_w, d_w_out, ffn_norm_g, ffn_w_up, ffn_conv_w, ffn_conv_b, ffn_w_down, final_norm_g):
    depth = ffn_norm_g.shape[0]
    n_mixers = 4
    for layer in range(depth):
        m, j = layer % n_mixers, layer // n_mixers
        if m == 0:
            block, name = _gmlp_block, "layer_gmlp"
            mc = _gmlp_consts(a_norm_g[j], a_w_in[j], a_b_in[j], a_v_norm_g[j], a_w_s[j], a_b_s[j], a_w_out[j])
            ms = _gmlp_scratch(a_w_out.shape[1])
        elif m == 1:
            block, name = _pool_block, "layer_pool"
            mc = _pool_consts(b_norm_g[j], b_w_in[j], b_w_grp[j], b_b_grp[j], b_scale[j], b_w_out[j])
            ms = _pool_scratch(b_w_out.shape[1])
        elif m == 2:
            block, name = _rglru_block, "layer_rglru"
            mc = _rglru_consts(c_norm_g[j], c_w_in[j], c_b_in[j], c_conv_w[j], c_conv_b[j], c_w_a[j], c_b_a[j],
                               c_w_i[j], c_b_i[j], c_lambda[j], c_w_out[j])
            ms = _rglru_scratch(c_w_out.shape[1], c_conv_w.shape[1])
        else:
            block, name = _sconv_block, "layer_sconv"
            mc = _sconv_consts(d_norm_g[j], d_w_in[j], d_conv_w[j], d_w_out[j])
            ms = _sconv_scratch(d_w_out.shape[1], d_conv_w.shape[1])
        fc = _ffn_consts(ffn_norm_g[layer], ffn_w_up[layer], ffn_conv_w[layer], ffn_conv_b[layer],
                         ffn_w_down[layer])
        x = _layer(x, block, mc, ms, fc, final_norm_g, first=(layer == 0), last=(layer == depth - 1), name=name)
    return x
```

```python
import functools

import jax
import jax.numpy as jnp
from jax import lax
from jax.experimental import pallas as pl
from jax.experimental.pallas import tpu as pltpu

EPS = 1e-6
SUB = 8
PB = 256
L = PB // SUB
C_GATE_C = 8.0
B_WINDOWS = (2, 4, 8, 16)
A_CHUNK = 128
FFN_CHUNK = 256
FFN_SLOTS = 4
VMEM_LIMIT = 56 * 1024 * 1024

F32 = jnp.float32
BF16 = jnp.bfloat16


def _to_perm(x):
    return x.reshape(SUB, L, x.shape[-1]).swapaxes(0, 1).reshape(x.shape)


def _from_perm(x):
    return x.reshape(L, SUB, x.shape[-1]).swapaxes(0, 1).reshape(x.shape)


def _block_time(shape, axis):
    p = lax.broadcasted_iota(jnp.int32, shape, axis)
    return (p & (SUB - 1)) * L + (p >> 3)


def _rms(x, g):
    ms = jnp.mean(x * x, axis=-1, keepdims=True)
    return x * lax.rsqrt(ms + EPS) * g


def _gelu(x):
    c = 0.7978845608028654
    half = 0.5 * x
    return half + half * jnp.tanh(x * (c + (c * 0.044715) * (x * x)))


def _dot(a, b):
    return jnp.dot(a, b, preferred_element_type=F32)


def _roll_tiles(t, k=1):
    parts = [pltpu.roll(t[r:r + SUB], k, 0) for r in range(0, t.shape[0], SUB)]
    return parts[0] if len(parts) == 1 else jnp.concatenate(parts, axis=0)


def _halo(cur_tail, prev_tail):
    first = (lax.broadcasted_iota(jnp.int32, cur_tail.shape, 0) & (SUB - 1)) == 0
    return jnp.where(first, _roll_tiles(prev_tail), _roll_tiles(cur_tail))


def _cat2(ref, a, b):
    return jnp.concatenate([ref[:, a], ref[:, b]], axis=1)


def _zero(*refs):
    for r in refs:
        r[...] = jnp.zeros(r.shape, r.dtype)


def _ffn_consts(g, w_up, conv_w, conv_b, w_down):
    d, f = w_down.shape[1], w_down.shape[0]
    return [g.reshape(1, d), w_up.astype(BF16), conv_w, conv_b.reshape(1, 2 * f), w_down.astype(BF16)]


def _ffn_scratch(f):
    hr = 2 * SUB
    return [pltpu.VMEM((FFN_SLOTS, PB + hr, 2 * FFN_CHUNK), F32), pltpu.VMEM((2, PB, f), BF16),
            pltpu.VMEM((hr, 2 * f), F32)]


def _ffn_block(x, n, blk, consts, scratch):
    g_ref, wup_ref, cw_ref, cb_ref, wdn_ref = consts
    zbuf, ybuf, tail = scratch
    f, cw, hr = wdn_ref.shape[0], FFN_CHUNK, 2 * SUB
    h = _rms(x, g_ref[...]).astype(BF16)
    for c in range(f // cw):
        gs = slice(c * cw, (c + 1) * cw)
        vs = slice(f + c * cw, f + (c + 1) * cw)
        zs = slice(2 * c * cw, 2 * (c + 1) * cw)
        zb = zbuf.at[(n * (f // cw) + c) % FFN_SLOTS]
        zb[hr:hr + PB, :] = _dot(h, _cat2(wup_ref, gs, vs))
        cur = zb[PB:PB + hr, :]
        zb[0:hr, :] = _halo(cur, tail[:, zs])
        tail[:, zs] = cur
        w = _cat2(cw_ref, gs, vs)
        zc = (zb[0:PB, :] * w[0:1] + zb[SUB:SUB + PB, :] * w[1:2]
              + zb[hr:hr + PB, :] * w[2:3] + _cat2(cb_ref, gs, vs))
        gt = zc[:, :cw]
        ybuf[n % 2, :, gs] = ((gt * jax.nn.sigmoid(gt)) * zc[:, cw:]).astype(BF16)
        yield
    return x + _dot(ybuf[n % 2], wdn_ref[...])


def _gmlp_consts(g, w_in, b_in, vg, w_s, b_s, w_out):
    d, aw = w_out.shape[1], w_out.shape[0]
    p = jnp.arange(PB)
    idx = ((p % SUB) * L + p // SUB) % A_CHUNK
    return [g.reshape(1, d), w_in.astype(BF16), b_in.reshape(1, 2 * aw), vg.reshape(1, aw),
            w_s[:, idx][:, :, idx], b_s[:, idx][..., None], w_out.astype(BF16)]


def _gmlp_scratch(aw):
    return []


def _gmlp_block(x, n, blk, consts, scratch):
    g_ref, win_ref, bin_ref, vg_ref, ws_ref, bs_ref, wout_ref = consts
    aw, groups = wout_ref.shape[0], ws_ref.shape[0]
    gw = aw // groups
    tp = _block_time((PB, PB), 0)
    tq = _block_time((PB, PB), 1)
    chunk_bits = A_CHUNK.bit_length() - 1
    keep = (tq <= tp) & ((tq >> chunk_bits) == (tp >> chunk_bits))
    h = _rms(x, g_ref[...]).astype(BF16)
    zvs = []
    for gi in range(groups):
        vs = slice(aw + gi * gw, aw + (gi + 1) * gw)
        zvs.append(_gelu(_dot(h, win_ref[:, vs]) + bin_ref[:, vs]))
        yield
    v = _rms(jnp.concatenate(zvs, axis=1), vg_ref[...]).astype(BF16)
    yield
    ys = []
    for gi in range(groups):
        cs = slice(gi * gw, (gi + 1) * gw)
        ws = jnp.where(keep, ws_ref[gi], 0.0).astype(BF16)
        vm = _dot(ws, v[:, cs]) + bs_ref[gi]
        zu = _gelu(_dot(h, win_ref[:, cs]) + bin_ref[:, cs])
        ys.append((zu * vm).astype(BF16))
        yield
    return x + _dot(jnp.concatenate(ys, axis=1), wout_ref[...])


POOL_HALO = 16 * SUB


def _pool_consts(g, w_in, w_grp, b_grp, scale, w_out):
    d, bw = w_out.shape[1], w_out.shape[0]
    return [g.reshape(1, d), w_in.astype(BF16), w_grp, b_grp, scale.reshape(1, bw), w_out.astype(BF16)]


def _pool_scratch(bw):
    return [pltpu.VMEM((POOL_HALO, bw), F32)]


def _pool_block(x, n, blk, consts, scratch):
    g_ref, win_ref, wg_ref, bg_ref, sc_ref, wout_ref = consts
    (tail,) = scratch
    bw, groups = wout_ref.shape[0], wg_ref.shape[0]
    gw = bw // groups
    h = _rms(x, g_ref[...]).astype(BF16)
    z = _dot(h, win_ref[...])
    yield
    cur = z[PB - POOL_HALO:PB, :]
    ext = jnp.concatenate([_halo(cur, tail[...]), z], axis=0)
    tail[...] = cur
    pos = blk * PB + _block_time((PB, 1), 0)
    ys = []
    for gi, win in enumerate(B_WINDOWS):
        yield
        cs = slice(gi * gw, (gi + 1) * gw)
        s = ext[:, cs]
        step = 1
        while step < win:
            s = s[step * SUB:] + s[:-step * SUB]
            step *= 2
        acc = s[s.shape[0] - PB:]
        inv_cnt = 1.0 / jnp.minimum(pos + 1, win).astype(F32)
        p = (acc * inv_cnt - z[:, cs]).astype(BF16)
        ys.append(_dot(p, wg_ref[gi].astype(BF16)) + bg_ref[gi:gi + 1, :])
    yield
    y = jnp.concatenate(ys, axis=1) * sc_ref[...]
    return x + _dot(y.astype(BF16), wout_ref[...])


def _rglru_consts(g, w_in, b_in, conv_w, conv_b, w_a, b_a, w_i, b_i, lam, w_out):
    d, cwid = w_out.shape[1], w_out.shape[0]
    return [g.reshape(1, d), w_in.astype(BF16), b_in.reshape(1, 2 * cwid), conv_w, conv_b.reshape(1, cwid),
            w_a, b_a, w_i, b_i, lam.reshape(1, cwid), w_out.astype(BF16)]


def _rglru_scratch(cwid, taps):
    hr = (taps - 1) * SUB
    return [pltpu.VMEM((PB + hr, cwid), F32), pltpu.VMEM((PB, cwid), F32), pltpu.VMEM((PB, cwid), F32),
            pltpu.VMEM((hr, cwid), F32), pltpu.VMEM((SUB, cwid), F32)]


def _rglru_block(x, n, blk, consts, scratch):
    g_ref, win_ref, bin_ref, cw_ref, cb_ref, wa_ref, ba_ref, wi_ref, bi_ref, lam_ref, wout_ref = consts
    xbuf, abuf, hbuf, tail, hcar = scratch
    cwid, heads, taps = wout_ref.shape[0], wa_ref.shape[0], cw_ref.shape[0]
    hr = (taps - 1) * SUB
    hw = cwid // heads
    sub = lax.broadcasted_iota(jnp.int32, (SUB, cwid), 0)
    nlam = -lam_ref[...]
    softplus = jnp.maximum(nlam, 0.0) + jnp.log1p(jnp.exp(-jnp.abs(nlam)))
    h = _rms(x, g_ref[...]).astype(BF16)
    yield
    xbuf[hr:hr + PB, :] = _dot(h, win_ref[:, cwid:2 * cwid]) + bin_ref[:, cwid:2 * cwid]
    cur = xbuf[PB:PB + hr, :]
    xbuf[0:hr, :] = _halo(cur, tail[...])
    tail[...] = cur
    yield
    xr = cb_ref[...] + xbuf[hr:hr + PB, :] * cw_ref[taps - 1:taps, :]
    for k in range(1, taps):
        xr = xr + xbuf[hr - k * SUB:hr - k * SUB + PB, :] * cw_ref[taps - 1 - k:taps - k, :]
    xrb = xr.astype(BF16)
    yield
    a_cols, b_cols = [], []
    for hd in range(heads):
        cs = slice(hd * hw, (hd + 1) * hw)
        w2 = jnp.concatenate([wa_ref[hd], wi_ref[hd]], axis=1).astype(BF16)
        both = _dot(xrb[:, cs], w2)
        r = jax.nn.sigmoid(both[:, :hw] + ba_ref[hd:hd + 1, :])
        ig = jax.nn.sigmoid(both[:, hw:] + bi_ref[hd:hd + 1, :])
        log_a = (-C_GATE_C * r) * softplus[:, cs]
        a_cols.append(jnp.exp(log_a))
        th = jnp.tanh(log_a)
        b_cols.append(jnp.sqrt(-2.0 * th / (1.0 - th)) * (ig * xr[:, cs]))
        yield
    a = jnp.concatenate(a_cols, axis=1)
    b = jnp.concatenate(b_cols, axis=1)

    acum = a[0:SUB]
    hloc = b[0:SUB]
    abuf[0:SUB, :] = acum
    hbuf[0:SUB, :] = hloc
    for j in range(1, L):
        aj = a[j * SUB:(j + 1) * SUB]
        hloc = aj * hloc + b[j * SUB:(j + 1) * SUB]
        acum = aj * acum
        abuf[j * SUB:(j + 1) * SUB, :] = acum
        hbuf[j * SUB:(j + 1) * SUB, :] = hloc
        if j % SUB == SUB - 1:
            yield
    for k in (1, 2, 4):
        a_sh = jnp.where(sub < k, 1.0, pltpu.roll(acum, k, 0))
        h_sh = jnp.where(sub < k, 0.0, pltpu.roll(hloc, k, 0))
        hloc = acum * h_sh + hloc
        acum = acum * a_sh
    c0 = hcar[...]
    ends = hloc + acum * c0
    cin = jnp.where(sub == 0, c0, pltpu.roll(ends, 1, 0))
    hcar[...] = jnp.broadcast_to(ends[SUB - 1:SUB, :], (SUB, cwid))
    hs = hbuf[...] + abuf[...] * jnp.concatenate([cin] * L, axis=0)
    yield
    gate = _gelu(_dot(h, win_ref[:, 0:cwid]) + bin_ref[:, 0:cwid])
    return x + _dot((hs * gate).astype(BF16), wout_ref[...])


def _sconv_consts(g, w_in, conv_w, w_out):
    d = w_out.shape[1]
    return [g.reshape(1, d), w_in.astype(BF16), conv_w, w_out.astype(BF16)]


def _sconv_scratch(dw, taps):
    hr = (taps - 1) * SUB
    return [pltpu.VMEM((PB + hr, dw), F32), pltpu.VMEM((hr, dw), F32)]


def _sconv_block(x, n, blk, consts, scratch):
    g_ref, win_ref, cw_ref, wout_ref = consts
    mbuf, tail = scratch
    dw, taps = wout_ref.shape[0], cw_ref.shape[0]
    hr = (taps - 1) * SUB
    h = _rms(x, g_ref[...]).astype(BF16)
    yield
    cg = _dot(h, win_ref[:, dw:2 * dw])
    yield
    mbuf[hr:hr + PB, :] = cg * _dot(h, win_ref[:, 2 * dw:3 * dw])
    cur = mbuf[PB:PB + hr, :]
    mbuf[0:hr, :] = _halo(cur, tail[...])
    tail[...] = cur
    yield
    cv = mbuf[hr:hr + PB, :] * cw_ref[taps - 1:taps, :]
    for k in range(1, taps):
        cv = cv + mbuf[hr - k * SUB:hr - k * SUB + PB, :] * cw_ref[taps - 1 - k:taps - k, :]
    yield
    y = (_dot(h, win_ref[:, 0:dw]) * cv).astype(BF16)
    yield
    return x + _dot(y, wout_ref[...])


def _trace_alternately(gens):
    results = [None] * len(gens)
    live = [True] * len(gens)
    while any(live):
        for k in range(len(gens)):
            if live[k]:
                try:
                    next(gens[k])
                except StopIteration as done:
                    results[k], live[k] = done.value, False
    return results
def _layer_body(*refs, mixer_block, n_mix, n_mix_scr, nb, nt, tiles, first, last):
    x_ref = refs[0]
    mix_consts = refs[1:1 + n_mix]
    ffn_consts = refs[1 + n_mix:1 + n_mix + 5]
    fg_ref = refs[1 + n_mix + 5]
    o_ref = refs[1 + n_mix + 6]
    scratch = refs[1 + n_mix + 7:]
    mid, mix_scr, ffn_scr = scratch[0], scratch[1:1 + n_mix_scr], scratch[1 + n_mix_scr:]
    t = pl.program_id(0)
    mix_tile = lax.rem(t, nt)
    ffn_tile = lax.rem(t + nt - 1, nt)

    @pl.when(mix_tile == 0)
    def _():
        _zero(*mix_scr[len(mix_scr) - _N_CARRY[mixer_block]:])

    @pl.when(ffn_tile == 0)
    def _():
        _zero(ffn_scr[-1])

    def run(do_ffn, do_mix):
        for n in range(nb):
            rows = slice(n * PB, (n + 1) * PB)
            gens = []
            if do_ffn:
                gens.append(_ffn_block(mid[rows, :], n, ffn_tile * nb + n, ffn_consts, ffn_scr))
            if do_mix:
                x = x_ref[0, rows, :]
                if first:
                    x = _to_perm(x)
                gens.append(mixer_block(x, n, mix_tile * nb + n, mix_consts, mix_scr))
            res = _trace_alternately(gens)
            if do_ffn:
                y = res[0]
                if last:
                    y = _from_perm(_rms(y, fg_ref[...]))
                o_ref[0, rows, :] = y
            if do_mix:
                mid[rows, :] = res[-1]

    pl.when(t == 0)(functools.partial(run, False, True))
    pl.when((t > 0) & (t < tiles))(functools.partial(run, True, True))
    pl.when(t == tiles)(functools.partial(run, True, False))


_N_CARRY = {_gmlp_block: 0, _pool_block: 1, _rglru_block: 2, _sconv_block: 1}


def _const_spec(shape):
    nd = len(shape)
    return pl.BlockSpec(shape, lambda t: (0,) * nd, pipeline_mode=pl.Buffered(1))


def _layer(x, mixer_block, mix_consts, mix_scratch, ffn_consts, final_g, first, last, name, tm=512):
    bsz, s, d = x.shape
    f = ffn_consts[-1].shape[0]
    nt = s // tm
    tiles = bsz * nt
    consts = list(mix_consts) + list(ffn_consts) + [final_g.reshape(1, d)]
    body = functools.partial(_layer_body, mixer_block=mixer_block, n_mix=len(mix_consts),
                             n_mix_scr=len(mix_scratch), nb=tm // PB, nt=nt, tiles=tiles, first=first, last=last)

    def in_map(t):
        tt = jnp.minimum(t, tiles - 1)
        return (lax.div(tt, nt), lax.rem(tt, nt), 0)

    def out_map(t):
        tt = jnp.maximum(t - 1, 0)
        return (lax.div(tt, nt), lax.rem(tt, nt), 0)

    return pl.pallas_call(
        body,
        grid=(tiles + 1,),
        in_specs=[pl.BlockSpec((1, tm, d), in_map)] + [_const_spec(c.shape) for c in consts],
        out_specs=pl.BlockSpec((1, tm, d), out_map),
        out_shape=jax.ShapeDtypeStruct(x.shape, x.dtype),
        scratch_shapes=[pltpu.VMEM((tm, d), F32)] + list(mix_scratch) + _ffn_scratch(f),
        compiler_params=pltpu.CompilerParams(
            dimension_semantics=("arbitrary",), vmem_limit_bytes=VMEM_LIMIT),
        name=name,
    )(x, *consts)


def kernel(x, a_norm_g, a_w_in, a_b_in, a_v_norm_g, a_w_s, a_b_s, a_w_out, b_norm_g, b_w_in, b_w_grp, b_b_grp, b_scale, b_w_out, c_norm_g, c_w_in, c_b_in, c_conv_w, c_conv_b, c_w_a, c_b_a, c_w_i, c_b_i, c_lambda, c_w_out, d_norm_g, d_w_in, d_conv_w, d_w_out, ffn_norm_g, ffn_w_up, ffn_conv_w, ffn_conv_b, ffn_w_down, final_norm_g):
    depth = ffn_norm_g.shape[0]
    n_mixers = 4
    for layer in range(depth):
        m, j = layer % n_mixers, layer // n_mixers
        if m == 0:
            block, name = _gmlp_block, "layer_gmlp"
            mc = _gmlp_consts(a_norm_g[j], a_w_in[j], a_b_in[j], a_v_norm_g[j], a_w_s[j], a_b_s[j], a_w_out[j])
            ms = _gmlp_scratch(a_w_out.shape[1])
        elif m == 1:
            block, name = _pool_block, "layer_pool"
            mc = _pool_consts(b_norm_g[j], b_w_in[j], b_w_grp[j], b_b_grp[j], b_scale[j], b_w_out[j])
            ms = _pool_scratch(b_w_out.shape[1])
        elif m == 2:
            block, name = _rglru_block, "layer_rglru"
            mc = _rglru_consts(c_norm_g[j], c_w_in[j], c_b_in[j], c_conv_w[j], c_conv_b[j], c_w_a[j], c_b_a[j],
                               c_w_i[j], c_b_i[j], c_lambda[j], c_w_out[j])
            ms = _rglru_scratch(c_w_out.shape[1], c_conv_w.shape[1])
        else:
            block, name = _sconv_block, "layer_sconv"
            mc = _sconv_consts(d_norm_g[j], d_w_in[j], d_conv_w[j], d_w_out[j])
            ms = _sconv_scratch(d_w_out.shape[1], d_conv_w.shape[1])
        fc = _ffn_consts(ffn_norm_g[layer], ffn_w_up[layer], ffn_conv_w[layer], ffn_conv_b[layer],
                         ffn_w_down[layer])
        x = _layer(x, block, mc, ms, fc, final_norm_g, first=(layer == 0), last=(layer == depth - 1), name=name)
    return x
```

```python
import functools

import jax
import jax.numpy as jnp
from jax import lax
from jax.experimental import pallas as pl
from jax.experimental.pallas import tpu as pltpu

EPS = 1e-6
SUB = 8
PB = 256
L = PB // SUB
C_GATE_C = 8.0
B_WINDOWS = (2, 4, 8, 16)
A_CHUNK = 128
FFN_CHUNK = 256
FFN_SLOTS = 4
VMEM_LIMIT = 56 * 1024 * 1024

F32 = jnp.float32
BF16 = jnp.bfloat16


def _to_perm(x):
    return x.reshape(SUB, L, x.shape[-1]).swapaxes(0, 1).reshape(x.shape)


def _from_perm(x):
    return x.reshape(L, SUB, x.shape[-1]).swapaxes(0, 1).reshape(x.shape)


def _block_time(shape, axis):
    p = lax.broadcasted_iota(jnp.int32, shape, axis)
    return (p & (SUB - 1)) * L + (p >> 3)


def _rms(x, g):
    ms = jnp.mean(x * x, axis=-1, keepdims=True)
    return x * lax.rsqrt(ms + EPS) * g


def _gelu(x):
    c = 0.7978845608028654
    half = 0.5 * x
    return half + half * jnp.tanh(x * (c + (c * 0.044715) * (x * x)))


def _dot(a, b):
    return jnp.dot(a, b, preferred_element_type=F32)


def _roll_tiles(t, k=1):
    parts = [pltpu.roll(t[r:r + SUB], k, 0) for r in range(0, t.shape[0], SUB)]
    return parts[0] if len(parts) == 1 else jnp.concatenate(parts, axis=0)


def _halo(cur_tail, prev_tail):
    first = (lax.broadcasted_iota(jnp.int32, cur_tail.shape, 0) & (SUB - 1)) == 0
    return jnp.where(first, _roll_tiles(prev_tail), _roll_tiles(cur_tail))


def _cat2(ref, a, b):
    return jnp.concatenate([ref[:, a], ref[:, b]], axis=1)


def _zero(*refs):
    for r in refs:
        r[...] = jnp.zeros(r.shape, r.dtype)


def _ffn_consts(g, w_up, conv_w, conv_b, w_down):
    depth, f, d = w_down.shape
    return [g.reshape(depth, 1, d), w_up.astype(BF16), conv_w, conv_b.reshape(depth, 1, 2 * f), w_down.astype(BF16)]


def _ffn_scratch(f):
    hr = 2 * SUB
    return [pltpu.VMEM((FFN_SLOTS, PB + hr, 2 * FFN_CHUNK), F32), pltpu.VMEM((2, PB, f), BF16),
            pltpu.VMEM((hr, 2 * f), F32)]


def _ffn_block(x, n, blk, consts, scratch):
    g_ref, wup_ref, cw_ref, cb_ref, wdn_ref = [r.at[0] for r in consts]
    zbuf, ybuf, tail = scratch
    f, cw, hr = wdn_ref.shape[0], FFN_CHUNK, 2 * SUB
    h = _rms(x, g_ref[...]).astype(BF16)
    for c in range(f // cw):
        gs = slice(c * cw, (c + 1) * cw)
        vs = slice(f + c * cw, f + (c + 1) * cw)
        zs = slice(2 * c * cw, 2 * (c + 1) * cw)
        zb = zbuf.at[(n * (f // cw) + c) % FFN_SLOTS]
        zb[hr:hr + PB, :] = _dot(h, _cat2(wup_ref, gs, vs))
        cur = zb[PB:PB + hr, :]
        zb[0:hr, :] = _halo(cur, tail[:, zs])
        tail[:, zs] = cur
        w = _cat2(cw_ref, gs, vs)
        zc = (zb[0:PB, :] * w[0:1] + zb[SUB:SUB + PB, :] * w[1:2]
              + zb[hr:hr + PB, :] * w[2:3] + _cat2(cb_ref, gs, vs))
        gt = zc[:, :cw]
        ybuf[n % 2, :, gs] = ((gt * jax.nn.sigmoid(gt)) * zc[:, cw:]).astype(BF16)
        yield
    return x + _dot(ybuf[n % 2], wdn_ref[...])


def _gmlp_consts(g, w_in, b_in, vg, w_s, b_s, w_out):
    d, aw = w_out.shape[1], w_out.shape[0]
    p = jnp.arange(PB)
    idx = ((p % SUB) * L + p // SUB) % A_CHUNK
    return [g.reshape(1, d), w_in.astype(BF16), b_in.reshape(1, 2 * aw), vg.reshape(1, aw),
            w_s[:, idx][:, :, idx], b_s[:, idx][..., None], w_out.astype(BF16)]


def _gmlp_scratch(aw):
    return []


def _gmlp_block(x, n, blk, consts, scratch):
    g_ref, win_ref, bin_ref, vg_ref, ws_ref, bs_ref, wout_ref = consts
    aw, groups = wout_ref.shape[0], ws_ref.shape[0]
    gw = aw // groups
    tp = _block_time((PB, PB), 0)
    tq = _block_time((PB, PB), 1)
    chunk_bits = A_CHUNK.bit_length() - 1
    keep = (tq <= tp) & ((tq >> chunk_bits) == (tp >> chunk_bits))
    h = _rms(x, g_ref[...]).astype(BF16)
    zvs = []
    for gi in range(groups):
        vs = slice(aw + gi * gw, aw + (gi + 1) * gw)
        zvs.append(_gelu(_dot(h, win_ref[:, vs]) + bin_ref[:, vs]))
        yield
    v = _rms(jnp.concatenate(zvs, axis=1), vg_ref[...]).astype(BF16)
    yield
    ys = []
    for gi in range(groups):
        cs = slice(gi * gw, (gi + 1) * gw)
        ws = jnp.where(keep, ws_ref[gi], 0.0).astype(BF16)
        vm = _dot(ws, v[:, cs]) + bs_ref[gi]
        zu = _gelu(_dot(h, win_ref[:, cs]) + bin_ref[:, cs])
        ys.append((zu * vm).astype(BF16))
        yield
    return x + _dot(jnp.concatenate(ys, axis=1), wout_ref[...])


POOL_HALO = 16 * SUB


def _pool_consts(g, w_in, w_grp, b_grp, scale, w_out):
    d, bw = w_out.shape[1], w_out.shape[0]
    return [g.reshape(1, d), w_in.astype(BF16), w_grp, b_grp, scale.reshape(1, bw), w_out.astype(BF16)]


def _pool_scratch(bw):
    return [pltpu.VMEM((POOL_HALO, bw), F32)]


def _pool_block(x, n, blk, consts, scratch):
    g_ref, win_ref, wg_ref, bg_ref, sc_ref, wout_ref = consts
    (tail,) = scratch
    bw, groups = wout_ref.shape[0], wg_ref.shape[0]
    gw = bw // groups
    h = _rms(x, g_ref[...]).astype(BF16)
    z = _dot(h, win_ref[...])
    yield
    cur = z[PB - POOL_HALO:PB, :]
    ext = jnp.concatenate([_halo(cur, tail[...]), z], axis=0)
    tail[...] = cur
    pos = blk * PB + _block_time((PB, 1), 0)
    ys = []
    for gi, win in enumerate(B_WINDOWS):
        yield
        cs = slice(gi * gw, (gi + 1) * gw)
        s = ext[:, cs]
        step = 1
        while step < win:
            s = s[step * SUB:] + s[:-step * SUB]
            step *= 2
        acc = s[s.shape[0] - PB:]
        inv_cnt = 1.0 / jnp.minimum(pos + 1, win).astype(F32)
        p = (acc * inv_cnt - z[:, cs]).astype(BF16)
        ys.append(_dot(p, wg_ref[gi].astype(BF16)) + bg_ref[gi:gi + 1, :])
    yield
    y = jnp.concatenate(ys, axis=1) * sc_ref[...]
    return x + _dot(y.astype(BF16), wout_ref[...])


def _rglru_consts(g, w_in, b_in, conv_w, conv_b, w_a, b_a, w_i, b_i, lam, w_out):
    d, cwid = w_out.shape[1], w_out.shape[0]
    return [g.reshape(1, d), w_in.astype(BF16), b_in.reshape(1, 2 * cwid), conv_w, conv_b.reshape(1, cwid),
            w_a, b_a, w_i, b_i, lam.reshape(1, cwid), w_out.astype(BF16)]


def _rglru_scratch(cwid, taps):
    hr = (taps - 1) * SUB
    return [pltpu.VMEM((PB + hr, cwid), F32), pltpu.VMEM((PB, cwid), F32), pltpu.VMEM((PB, cwid), F32),
            pltpu.VMEM((hr, cwid), F32), pltpu.VMEM((SUB, cwid), F32)]


def _rglru_block(x, n, blk, consts, scratch):
    g_ref, win_ref, bin_ref, cw_ref, cb_ref, wa_ref, ba_ref, wi_ref, bi_ref, lam_ref, wout_ref = consts
    xbuf, abuf, hbuf, tail, hcar = scratch
    cwid, heads, taps = wout_ref.shape[0], wa_ref.shape[0], cw_ref.shape[0]
    hr = (taps - 1) * SUB
    hw = cwid // heads
    sub = lax.broadcasted_iota(jnp.int32, (SUB, cwid), 0)
    nlam = -lam_ref[...]
    softplus = jnp.maximum(nlam, 0.0) + jnp.log1p(jnp.exp(-jnp.abs(nlam)))
    h = _rms(x, g_ref[...]).astype(BF16)
    yield
    xbuf[hr:hr + PB, :] = _dot(h, win_ref[:, cwid:2 * cwid]) + bin_ref[:, cwid:2 * cwid]
    cur = xbuf[PB:PB + hr, :]
    xbuf[0:hr, :] = _halo(cur, tail[...])
    tail[...] = cur
    yield
    xr = cb_ref[...] + xbuf[hr:hr + PB, :] * cw_ref[taps - 1:taps, :]
    for k in range(1, taps):
        xr = xr + xbuf[hr - k * SUB:hr - k * SUB + PB, :] * cw_ref[taps - 1 - k:taps - k, :]
    xrb = xr.astype(BF16)
    yield
    a_cols, b_cols = [], []
    for hd in range(heads):
        cs = slice(hd * hw, (hd + 1) * hw)
        w2 = jnp.concatenate([wa_ref[hd], wi_ref[hd]], axis=1).astype(BF16)
        both = _dot(xrb[:, cs], w2)
        r = jax.nn.sigmoid(both[:, :hw] + ba_ref[hd:hd + 1, :])
        ig = jax.nn.sigmoid(both[:, hw:] + bi_ref[hd:hd + 1, :])
        log_a = (-C_GATE_C * r) * softplus[:, cs]
        a_cols.append(jnp.exp(log_a))
        th = jnp.tanh(log_a)
        b_cols.append(jnp.sqrt(-2.0 * th / (1.0 - th)) * (ig * xr[:, cs]))
        yield
    a = jnp.concatenate(a_cols, axis=1)
    b = jnp.concatenate(b_cols, axis=1)

    acum = a[0:SUB]
    hloc = b[0:SUB]
    abuf[0:SUB, :] = acum
    hbuf[0:SUB, :] = hloc
    for j in range(1, L):
        aj = a[j * SUB:(j + 1) * SUB]
        hloc = aj * hloc + b[j * SUB:(j + 1) * SUB]
        acum = aj * acum
        abuf[j * SUB:(j + 1) * SUB, :] = acum
        hbuf[j * SUB:(j + 1) * SUB, :] = hloc
        if j % SUB == SUB - 1:
            yield
    for k in (1, 2, 4):
        a_sh = jnp.where(sub < k, 1.0, pltpu.roll(acum, k, 0))
        h_sh = jnp.where(sub < k, 0.0, pltpu.roll(hloc, k, 0))
        hloc = acum * h_sh + hloc
        acum = acum * a_sh
    c0 = hcar[...]
    ends = hloc + acum * c0
    cin = jnp.where(sub == 0, c0, pltpu.roll(ends, 1, 0))
    hcar[...] = jnp.broadcast_to(ends[SUB - 1:SUB, :], (SUB, cwid))
    hs = hbuf[...] + abuf[...] * jnp.concatenate([cin] * L, axis=0)
    yield
    gate = _gelu(_dot(h, win_ref[:, 0:cwid]) + bin_ref[:, 0:cwid])
    return x + _dot((hs * gate).astype(BF16), wout_ref[...])


def _sconv_consts(g, w_in, conv_w, w_out):
    d = w_out.shape[1]
    return [g.reshape(1, d), w_in.astype(BF16), conv_w, w_out.astype(BF16)]


def _sconv_scratch(dw, taps):
    hr = (taps - 1) * SUB
    return [pltpu.VMEM((PB + hr, dw), F32), pltpu.VMEM((hr, dw), F32)]


def _sconv_block(x, n, blk, consts, scratch):
    g_ref, win_ref, cw_ref, wout_ref = consts
    mbuf, tail = scratch
    dw, taps = wout_ref.shape[0], cw_ref.shape[0]
    hr = (taps - 1) * SUB
    h = _rms(x, g_ref[...]).astype(BF16)
    yield
    cg = _dot(h, win_ref[:, dw:2 * dw])
    yield
    mbuf[hr:hr + PB, :] = cg * _dot(h, win_ref[:, 2 * dw:3 * dw])
    cur = mbuf[PB:PB + hr, :]
    mbuf[0:hr, :] = _halo(cur, tail[...])
    tail[...] = cur
    yield
    cv = mbuf[hr:hr + PB, :] * cw_ref[taps - 1:taps, :]
    for k in range(1, taps):
        cv = cv + mbuf[hr - k * SUB:hr - k * SUB + PB, :] * cw_ref[taps - 1 - k:taps - k, :]
    yield
    y = (_dot(h, win_ref[:, 0:dw]) * cv).astype(BF16)
    yield
    return x + _dot(y, wout_ref[...])


def _trace_alternately(gens):
    results = [None] * len(gens)
    live = [True] * len(gens)
    while any(live):
        for k in range(len(gens)):
            if live[k]:
                try:
                    next(gens[k])
                except StopIteration as done:
                    results[k], live[k] = done.value, False
    return results
def _layer_body(*refs, mixer_block, n_mix, n_mix_scr, nb, nt, first, last):
    x_ref = refs[0]
    mix_consts = refs[1:1 + n_mix]
    ffn_consts = refs[1 + n_mix:1 + n_mix + 5]
    fg_ref = refs[1 + n_mix + 5]
    o_ref = refs[1 + n_mix + 6]
    scratch = refs[1 + n_mix + 7:]
    mid, mix_scr, ffn_scr = scratch[0], scratch[1:1 + n_mix_scr], scratch[1 + n_mix_scr:]
    t = pl.program_id(0)
    mix_tile = lax.rem(t, nt)
    ffn_tile = lax.rem(t + nt - 1, nt)

    @pl.when(t == 0)
    def _():
        _zero(mid)

    @pl.when(mix_tile == 0)
    def _():
        _zero(*mix_scr[len(mix_scr) - _N_CARRY[mixer_block]:])

    @pl.when((ffn_tile == 0) | (t == 0))
    def _():
        _zero(ffn_scr[-1])

    for n in range(nb):
        rows = slice(n * PB, (n + 1) * PB)
        x = x_ref[0, rows, :]
        if first:
            x = _to_perm(x)
        y, m = _trace_alternately([_ffn_block(mid[rows, :], n, ffn_tile * nb + n, ffn_consts, ffn_scr),
                                   mixer_block(x, n, mix_tile * nb + n, mix_consts, mix_scr)])
        if last:
            y = _from_perm(_rms(y, fg_ref[...]))
        o_ref[0, rows, :] = y
        mid[rows, :] = m


_N_CARRY = {_gmlp_block: 0, _pool_block: 1, _rglru_block: 2, _sconv_block: 1}


def _const_spec(shape):
    nd = len(shape)
    return pl.BlockSpec(shape, lambda t: (0,) * nd, pipeline_mode=pl.Buffered(1))


def _layer_spec(shape, layer):
    nd = len(shape)
    return pl.BlockSpec((1,) + tuple(shape[1:]), lambda t: (layer,) + (0,) * (nd - 1),
                        pipeline_mode=pl.Buffered(1))


def _layer(x, mixer_block, mix_consts, mix_scratch, ffn_consts, layer, final_g, first, last, name, tm=512):
    bsz, s, d = x.shape
    f = ffn_consts[-1].shape[1]
    nt = s // tm
    tiles = bsz * nt
    consts = list(mix_consts) + list(ffn_consts) + [final_g.reshape(1, d)]
    const_specs = ([_const_spec(c.shape) for c in mix_consts] + [_layer_spec(c.shape, layer) for c in ffn_consts]
                   + [_const_spec((1, d))])
    body = functools.partial(_layer_body, mixer_block=mixer_block, n_mix=len(mix_consts),
                             n_mix_scr=len(mix_scratch), nb=tm // PB, nt=nt, first=first, last=last)

    def in_map(t):
        tt = jnp.minimum(t, tiles - 1)
        return (lax.div(tt, nt), lax.rem(tt, nt), 0)

    def out_map(t):
        tt = jnp.maximum(t - 1, 0)
        return (lax.div(tt, nt), lax.rem(tt, nt), 0)

    return pl.pallas_call(
        body,
        grid=(tiles + 1,),
        in_specs=[pl.BlockSpec((1, tm, d), in_map)] + const_specs,
        out_specs=pl.BlockSpec((1, tm, d), out_map),
        out_shape=jax.ShapeDtypeStruct(x.shape, x.dtype),
        scratch_shapes=[pltpu.VMEM((tm, d), F32)] + list(mix_scratch) + _ffn_scratch(f),
        compiler_params=pltpu.CompilerParams(
            dimension_semantics=("arbitrary",), vmem_limit_bytes=VMEM_LIMIT),
        name=name,
    )(x, *consts)


def kernel(x, a_norm_g, a_w_in, a_b_in, a_v_norm_g, a_w_s, a_b_s, a_w_out, b_norm_g, b_w_in, b_w_grp, b_b_grp, b_scale, b_w_out, c_norm_g, c_w_in, c_b_in, c_conv_w, c_conv_b, c_w_a, c_b_a, c_w_i, c_b_i, c_lambda, c_w_out, d_norm_g, d_w_in, d_conv_w, d_w_out, ffn_norm_g, ffn_w_up, ffn_conv_w, ffn_conv_b, ffn_w_down, final_norm_g):
    depth = ffn_norm_g.shape[0]
    n_mixers = 4
    fc = _ffn_consts(ffn_norm_g, ffn_w_up, ffn_conv_w, ffn_conv_b, ffn_w_down)
    for layer in range(depth):
        m, j = layer % n_mixers, layer // n_mixers
        if m == 0:
            block, name = _gmlp_block, "layer_gmlp"
            mc = _gmlp_consts(a_norm_g[j], a_w_in[j], a_b_in[j], a_v_norm_g[j], a_w_s[j], a_b_s[j], a_w_out[j])
            ms = _gmlp_scratch(a_w_out.shape[1])
        elif m == 1:
            block, name = _pool_block, "layer_pool"
            mc = _pool_consts(b_norm_g[j], b_w_in[j], b_w_grp[j], b_b_grp[j], b_scale[j], b_w_out[j])
            ms = _pool_scratch(b_w_out.shape[1])
        elif m == 2:
            block, name = _rglru_block, "layer_rglru"
            mc = _rglru_consts(c_norm_g[j], c_w_in[j], c_b_in[j], c_conv_w[j], c_conv_b[j], c_w_a[j], c_b_a[j],
                               c_w_i[j], c_b_i[j], c_lambda[j], c_w_out[j])
            ms = _rglru_scratch(c_w_out.shape[1], c_conv_w.shape[1])
        else:
            block, name = _sconv_block, "layer_sconv"
            mc = _sconv_consts(d_norm_g[j], d_w_in[j], d_conv_w[j], d_w_out[j])
            ms = _sconv_scratch(d_w_out.shape[1], d_conv_w.shape[1])
        x = _layer(x, block, mc, ms, fc, layer, final_norm_g, first=(layer == 0), last=(layer == depth - 1),
                   name=name)
    return x
```

```python
import functools

import jax
import jax.numpy as jnp
from jax import lax
from jax.experimental import pallas as pl
from jax.experimental.pallas import tpu as pltpu

EPS = 1e-6
SUB = 8
PB = 256
L = PB // SUB
C_GATE_C = 8.0
B_WINDOWS = (2, 4, 8, 16)
A_CHUNK = 128
FFN_CHUNK = 256
FFN_SLOTS = 4
VMEM_LIMIT = 56 * 1024 * 1024

F32 = jnp.float32
BF16 = jnp.bfloat16


def _to_perm(x):
    return x.reshape(SUB, L, x.shape[-1]).swapaxes(0, 1).reshape(x.shape)


def _from_perm(x):
    return x.reshape(L, SUB, x.shape[-1]).swapaxes(0, 1).reshape(x.shape)


def _block_time(shape, axis):
    p = lax.broadcasted_iota(jnp.int32, shape, axis)
    return (p & (SUB - 1)) * L + (p >> 3)


def _rms(x, g):
    ms = jnp.mean(x * x, axis=-1, keepdims=True)
    return x * lax.rsqrt(ms + EPS) * g


def _gelu(x):
    c = 0.7978845608028654
    half = 0.5 * x
    return half + half * jnp.tanh(x * (c + (c * 0.044715) * (x * x)))


def _dot(a, b):
    return jnp.dot(a, b, preferred_element_type=F32)


def _roll_tiles(t, k=1):
    parts = [pltpu.roll(t[r:r + SUB], k, 0) for r in range(0, t.shape[0], SUB)]
    return parts[0] if len(parts) == 1 else jnp.concatenate(parts, axis=0)


def _halo(cur_tail, prev_tail):
    first = (lax.broadcasted_iota(jnp.int32, cur_tail.shape, 0) & (SUB - 1)) == 0
    return jnp.where(first, _roll_tiles(prev_tail), _roll_tiles(cur_tail))


def _cat2(ref, a, b):
    return jnp.concatenate([ref[:, a], ref[:, b]], axis=1)


def _zero(*refs):
    for r in refs:
        r[...] = jnp.zeros(r.shape, r.dtype)


def _ffn_consts(g, w_up_bf16, conv_w, conv_b, w_down_bf16):
    depth, _, two_f = conv_w.shape
    return [g.reshape(depth, 1, g.shape[-1]), w_up_bf16, conv_w, conv_b.reshape(depth, 1, two_f), w_down_bf16]


def _ffn_scratch(f):
    hr = 2 * SUB
    return [pltpu.VMEM((FFN_SLOTS, PB + hr, 2 * FFN_CHUNK), F32), pltpu.VMEM((2, PB, f), BF16),
            pltpu.VMEM((hr, 2 * f), F32)]


def _ffn_block(x, n, blk, consts, scratch):
    g_ref, wup_ref, cw_ref, cb_ref, wdn_ref = [r.at[0] for r in consts]
    zbuf, ybuf, tail = scratch
    f, cw, hr = wdn_ref.shape[0], FFN_CHUNK, 2 * SUB
    h = _rms(x, g_ref[...]).astype(BF16)
    for c in range(f // cw):
        gs = slice(c * cw, (c + 1) * cw)
        vs = slice(f + c * cw, f + (c + 1) * cw)
        zs = slice(2 * c * cw, 2 * (c + 1) * cw)
        zb = zbuf.at[(n * (f // cw) + c) % FFN_SLOTS]
        zb[hr:hr + PB, :] = _dot(h, _cat2(wup_ref, gs, vs))
        cur = zb[PB:PB + hr, :]
        zb[0:hr, :] = _halo(cur, tail[:, zs])
        tail[:, zs] = cur
        w = _cat2(cw_ref, gs, vs)
        zc = (zb[0:PB, :] * w[0:1] + zb[SUB:SUB + PB, :] * w[1:2]
              + zb[hr:hr + PB, :] * w[2:3] + _cat2(cb_ref, gs, vs))
        gt = zc[:, :cw]
        ybuf[n % 2, :, gs] = ((gt * jax.nn.sigmoid(gt)) * zc[:, cw:]).astype(BF16)
        yield
    return x + _dot(ybuf[n % 2], wdn_ref[...])


def _gmlp_consts(g, w_in, b_in, vg, w_s, b_s, w_out):
    d, aw = w_out.shape[1], w_out.shape[0]
    p = jnp.arange(PB)
    idx = ((p % SUB) * L + p // SUB) % A_CHUNK
    return [g.reshape(1, d), w_in.astype(BF16), b_in.reshape(1, 2 * aw), vg.reshape(1, aw),
            w_s[:, idx][:, :, idx], b_s[:, idx][..., None], w_out.astype(BF16)]


def _gmlp_scratch(aw):
    return []


def _gmlp_block(x, n, blk, consts, scratch):
    g_ref, win_ref, bin_ref, vg_ref, ws_ref, bs_ref, wout_ref = consts
    aw, groups = wout_ref.shape[0], ws_ref.shape[0]
    gw = aw // groups
    tp = _block_time((PB, PB), 0)
    tq = _block_time((PB, PB), 1)
    chunk_bits = A_CHUNK.bit_length() - 1
    keep = (tq <= tp) & ((tq >> chunk_bits) == (tp >> chunk_bits))
    h = _rms(x, g_ref[...]).astype(BF16)
    zvs = []
    for gi in range(groups):
        vs = slice(aw + gi * gw, aw + (gi + 1) * gw)
        zvs.append(_gelu(_dot(h, win_ref[:, vs]) + bin_ref[:, vs]))
        yield
    v = _rms(jnp.concatenate(zvs, axis=1), vg_ref[...]).astype(BF16)
    yield
    ys = []
    for gi in range(groups):
        cs = slice(gi * gw, (gi + 1) * gw)
        ws = jnp.where(keep, ws_ref[gi], 0.0).astype(BF16)
        vm = _dot(ws, v[:, cs]) + bs_ref[gi]
        zu = _gelu(_dot(h, win_ref[:, cs]) + bin_ref[:, cs])
        ys.append((zu * vm).astype(BF16))
        yield
    return x + _dot(jnp.concatenate(ys, axis=1), wout_ref[...])


POOL_HALO = 16 * SUB


def _pool_consts(g, w_in, w_grp, b_grp, scale, w_out):
    d, bw = w_out.shape[1], w_out.shape[0]
    return [g.reshape(1, d), w_in.astype(BF16), w_grp, b_grp, scale.reshape(1, bw), w_out.astype(BF16)]


def _pool_scratch(bw):
    return [pltpu.VMEM((POOL_HALO, bw), F32)]


def _pool_block(x, n, blk, consts, scratch):
    g_ref, win_ref, wg_ref, bg_ref, sc_ref, wout_ref = consts
    (tail,) = scratch
    bw, groups = wout_ref.shape[0], wg_ref.shape[0]
    gw = bw // groups
    h = _rms(x, g_ref[...]).astype(BF16)
    z = _dot(h, win_ref[...])
    yield
    cur = z[PB - POOL_HALO:PB, :]
    ext = jnp.concatenate([_halo(cur, tail[...]), z], axis=0)
    tail[...] = cur
    pos = blk * PB + _block_time((PB, 1), 0)
    ys = []
    for gi, win in enumerate(B_WINDOWS):
        yield
        cs = slice(gi * gw, (gi + 1) * gw)
        s = ext[:, cs]
        step = 1
        while step < win:
            s = s[step * SUB:] + s[:-step * SUB]
            step *= 2
        acc = s[s.shape[0] - PB:]
        inv_cnt = 1.0 / jnp.minimum(pos + 1, win).astype(F32)
        p = (acc * inv_cnt - z[:, cs]).astype(BF16)
        ys.append(_dot(p, wg_ref[gi].astype(BF16)) + bg_ref[gi:gi + 1, :])
    yield
    y = jnp.concatenate(ys, axis=1) * sc_ref[...]
    return x + _dot(y.astype(BF16), wout_ref[...])


def _rglru_consts(g, w_in, b_in, conv_w, conv_b, w_a, b_a, w_i, b_i, lam, w_out):
    d, cwid = w_out.shape[1], w_out.shape[0]
    return [g.reshape(1, d), w_in.astype(BF16), b_in.reshape(1, 2 * cwid), conv_w, conv_b.reshape(1, cwid),
            w_a, b_a, w_i, b_i, lam.reshape(1, cwid), w_out.astype(BF16)]


def _rglru_scratch(cwid, taps):
    hr = (taps - 1) * SUB
    return [pltpu.VMEM((PB + hr, cwid), F32), pltpu.VMEM((PB, cwid), F32), pltpu.VMEM((PB, cwid), F32),
            pltpu.VMEM((hr, cwid), F32), pltpu.VMEM((SUB, cwid), F32)]


def _rglru_block(x, n, blk, consts, scratch):
    g_ref, win_ref, bin_ref, cw_ref, cb_ref, wa_ref, ba_ref, wi_ref, bi_ref, lam_ref, wout_ref = consts
    xbuf, abuf, hbuf, tail, hcar = scratch
    cwid, heads, taps = wout_ref.shape[0], wa_ref.shape[0], cw_ref.shape[0]
    hr = (taps - 1) * SUB
    hw = cwid // heads
    sub = lax.broadcasted_iota(jnp.int32, (SUB, cwid), 0)
    nlam = -lam_ref[...]
    softplus = jnp.maximum(nlam, 0.0) + jnp.log1p(jnp.exp(-jnp.abs(nlam)))
    h = _rms(x, g_ref[...]).astype(BF16)
    yield
    xbuf[hr:hr + PB, :] = _dot(h, win_ref[:, cwid:2 * cwid]) + bin_ref[:, cwid:2 * cwid]
    cur = xbuf[PB:PB + hr, :]
    xbuf[0:hr, :] = _halo(cur, tail[...])
    tail[...] = cur
    yield
    xr = cb_ref[...] + xbuf[hr:hr + PB, :] * cw_ref[taps - 1:taps, :]
    for k in range(1, taps):
        xr = xr + xbuf[hr - k * SUB:hr - k * SUB + PB, :] * cw_ref[taps - 1 - k:taps - k, :]
    xrb = xr.astype(BF16)
    yield
    a_cols, b_cols = [], []
    for hd in range(heads):
        cs = slice(hd * hw, (hd + 1) * hw)
        w2 = jnp.concatenate([wa_ref[hd], wi_ref[hd]], axis=1).astype(BF16)
        both = _dot(xrb[:, cs], w2)
        r = jax.nn.sigmoid(both[:, :hw] + ba_ref[hd:hd + 1, :])
        ig = jax.nn.sigmoid(both[:, hw:] + bi_ref[hd:hd + 1, :])
        log_a = (-C_GATE_C * r) * softplus[:, cs]
        a_cols.append(jnp.exp(log_a))
        th = jnp.tanh(log_a)
        b_cols.append(jnp.sqrt(-2.0 * th / (1.0 - th)) * (ig * xr[:, cs]))
        yield
    a = jnp.concatenate(a_cols, axis=1)
    b = jnp.concatenate(b_cols, axis=1)

    acum = a[0:SUB]
    hloc = b[0:SUB]
    abuf[0:SUB, :] = acum
    hbuf[0:SUB, :] = hloc
    for j in range(1, L):
        aj = a[j * SUB:(j + 1) * SUB]
        hloc = aj * hloc + b[j * SUB:(j + 1) * SUB]
        acum = aj * acum
        abuf[j * SUB:(j + 1) * SUB, :] = acum
        hbuf[j * SUB:(j + 1) * SUB, :] = hloc
        if j % SUB == SUB - 1:
            yield
    for k in (1, 2, 4):
        a_sh = jnp.where(sub < k, 1.0, pltpu.roll(acum, k, 0))
        h_sh = jnp.where(sub < k, 0.0, pltpu.roll(hloc, k, 0))
        hloc = acum * h_sh + hloc
        acum = acum * a_sh
    c0 = hcar[...]
    ends = hloc + acum * c0
    cin = jnp.where(sub == 0, c0, pltpu.roll(ends, 1, 0))
    hcar[...] = jnp.broadcast_to(ends[SUB - 1:SUB, :], (SUB, cwid))
    hs = hbuf[...] + abuf[...] * jnp.concatenate([cin] * L, axis=0)
    yield
    gate = _gelu(_dot(h, win_ref[:, 0:cwid]) + bin_ref[:, 0:cwid])
    return x + _dot((hs * gate).astype(BF16), wout_ref[...])


def _sconv_consts(g, w_in, conv_w, w_out):
    d = w_out.shape[1]
    return [g.reshape(1, d), w_in.astype(BF16), conv_w, w_out.astype(BF16)]


def _sconv_scratch(dw, taps):
    hr = (taps - 1) * SUB
    return [pltpu.VMEM((PB + hr, dw), F32), pltpu.VMEM((hr, dw), F32)]


def _sconv_block(x, n, blk, consts, scratch):
    g_ref, win_ref, cw_ref, wout_ref = consts
    mbuf, tail = scratch
    dw, taps = wout_ref.shape[0], cw_ref.shape[0]
    hr = (taps - 1) * SUB
    h = _rms(x, g_ref[...]).astype(BF16)
    yield
    cg = _dot(h, win_ref[:, dw:2 * dw])
    yield
    mbuf[hr:hr + PB, :] = cg * _dot(h, win_ref[:, 2 * dw:3 * dw])
    cur = mbuf[PB:PB + hr, :]
    mbuf[0:hr, :] = _halo(cur, tail[...])
    tail[...] = cur
    yield
    cv = mbuf[hr:hr + PB, :] * cw_ref[taps - 1:taps, :]
    for k in range(1, taps):
        cv = cv + mbuf[hr - k * SUB:hr - k * SUB + PB, :] * cw_ref[taps - 1 - k:taps - k, :]
    yield
    y = (_dot(h, win_ref[:, 0:dw]) * cv).astype(BF16)
    yield
    return x + _dot(y, wout_ref[...])


def _trace_alternately(gens):
    results = [None] * len(gens)
    live = [True] * len(gens)
    while any(live):
        for k in range(len(gens)):
            if live[k]:
                try:
                    next(gens[k])
                except StopIteration as done:
                    results[k], live[k] = done.value, False
    return results
def _layer_body(*refs, mixer_block, n_mix, n_mix_scr, n_cast, nb, nt, first, last):
    x_ref = refs[0]
    mix_consts = refs[1:1 + n_mix]
    ffn_consts = refs[1 + n_mix:1 + n_mix + 5]
    fg_ref = refs[1 + n_mix + 5]
    cast_src = refs[1 + n_mix + 6:1 + n_mix + 6 + n_cast]
    o_ref = refs[1 + n_mix + 6 + n_cast]
    cast_dst = refs[2 + n_mix + 6 + n_cast:2 + n_mix + 6 + 2 * n_cast]
    scratch = refs[2 + n_mix + 6 + 2 * n_cast:]
    mid, mix_scr, ffn_scr = scratch[0], scratch[1:1 + n_mix_scr], scratch[1 + n_mix_scr:]
    t = pl.program_id(0)

    for src, dst in zip(cast_src, cast_dst):
        dst[...] = src[...].astype(BF16)

    mix_tile = lax.rem(t, nt)
    ffn_tile = lax.rem(t + nt - 1, nt)

    @pl.when(t == 0)
    def _():
        _zero(mid)

    @pl.when(mix_tile == 0)
    def _():
        _zero(*mix_scr[len(mix_scr) - _N_CARRY[mixer_block]:])

    @pl.when((ffn_tile == 0) | (t == 0))
    def _():
        _zero(ffn_scr[-1])

    for n in range(nb):
        rows = slice(n * PB, (n + 1) * PB)
        x = x_ref[0, rows, :]
        if first:
            x = _to_perm(x)
        y, m = _trace_alternately([_ffn_block(mid[rows, :], n, ffn_tile * nb + n, ffn_consts, ffn_scr),
                                   mixer_block(x, n, mix_tile * nb + n, mix_consts, mix_scr)])
        if last:
            y = _from_perm(_rms(y, fg_ref[...]))
        o_ref[0, rows, :] = y
        mid[rows, :] = m


_N_CARRY = {_gmlp_block: 0, _pool_block: 1, _rglru_block: 2, _sconv_block: 1}


def _const_spec(shape):
    nd = len(shape)
    return pl.BlockSpec(shape, lambda t: (0,) * nd, pipeline_mode=pl.Buffered(1))


def _layer_spec(shape, layer):
    nd = len(shape)
    return pl.BlockSpec((1,) + tuple(shape[1:]), lambda t: (layer,) + (0,) * (nd - 1),
                        pipeline_mode=pl.Buffered(1))


BF16_ROWS = 16


def _layer(x, mixer_block, mix_consts, mix_scratch, ffn_consts, ffn_idx, to_cast, final_g, first, last, name,
           tm=512):
    bsz, s, d = x.shape
    f = ffn_consts[-1].shape[1]
    nt = s // tm
    tiles = bsz * nt
    consts = list(mix_consts) + list(ffn_consts) + [final_g.reshape(1, d)] + [w for w, _ in to_cast]
    const_specs = ([_const_spec(c.shape) for c in mix_consts]
                   + [_layer_spec(c.shape, k) for c, k in zip(ffn_consts, ffn_idx)] + [_const_spec((1, d))])
    cast_in, cast_out, cast_shapes = [], [], []
    for w, k in to_cast:
        n_rows, cols = w.shape[1:]
        rows = next(r for r in range(BF16_ROWS, n_rows + 1, BF16_ROWS)
                    if n_rows % r == 0 and n_rows // r <= tiles + 1)
        bands = n_rows // rows
        cast_in.append(pl.BlockSpec((1, rows, cols), lambda t, k=k, bands=bands: (k, jnp.minimum(t, bands - 1), 0)))
        cast_out.append(pl.BlockSpec((1, rows, cols), lambda t, bands=bands: (0, jnp.minimum(t, bands - 1), 0)))
        cast_shapes.append(jax.ShapeDtypeStruct((1, n_rows, cols), BF16))
    body = functools.partial(_layer_body, mixer_block=mixer_block, n_mix=len(mix_consts),
                             n_mix_scr=len(mix_scratch), n_cast=len(to_cast), nb=tm // PB, nt=nt,
                             first=first, last=last)

    def in_map(t):
        tt = jnp.minimum(t, tiles - 1)
        return (lax.div(tt, nt), lax.rem(tt, nt), 0)

    def out_map(t):
        tt = jnp.maximum(t - 1, 0)
        return (lax.div(tt, nt), lax.rem(tt, nt), 0)

    outs = pl.pallas_call(
        body,
        grid=(tiles + 1,),
        in_specs=[pl.BlockSpec((1, tm, d), in_map)] + const_specs + cast_in,
        out_specs=[pl.BlockSpec((1, tm, d), out_map)] + cast_out,
        out_shape=[jax.ShapeDtypeStruct(x.shape, x.dtype)] + cast_shapes,
        scratch_shapes=[pltpu.VMEM((tm, d), F32)] + list(mix_scratch) + _ffn_scratch(f),
        compiler_params=pltpu.CompilerParams(
            dimension_semantics=("arbitrary",), vmem_limit_bytes=VMEM_LIMIT),
        name=name,
    )(x, *consts)
    return outs[0], outs[1:]


def kernel(x, a_norm_g, a_w_in, a_b_in, a_v_norm_g, a_w_s, a_b_s, a_w_out, b_norm_g, b_w_in, b_w_grp, b_b_grp, b_scale, b_w_out, c_norm_g, c_w_in, c_b_in, c_conv_w, c_conv_b, c_w_a, c_b_a, c_w_i, c_b_i, c_lambda, c_w_out, d_norm_g, d_w_in, d_conv_w, d_w_out, ffn_norm_g, ffn_w_up, ffn_conv_w, ffn_conv_b, ffn_w_down, final_norm_g):
    depth = ffn_norm_g.shape[0]
    n_mixers = 4
    w_up, w_down = ffn_w_up[0:1].astype(BF16), ffn_w_down[0:1].astype(BF16)
    for layer in range(depth):
        m, j = layer % n_mixers, layer // n_mixers
        if m == 0:
            block, name = _gmlp_block, "layer_gmlp"
            mc = _gmlp_consts(a_norm_g[j], a_w_in[j], a_b_in[j], a_v_norm_g[j], a_w_s[j], a_b_s[j], a_w_out[j])
            ms = _gmlp_scratch(a_w_out.shape[1])
        elif m == 1:
            block, name = _pool_block, "layer_pool"
            mc = _pool_consts(b_norm_g[j], b_w_in[j], b_w_grp[j], b_b_grp[j], b_scale[j], b_w_out[j])
            ms = _pool_scratch(b_w_out.shape[1])
        elif m == 2:
            block, name = _rglru_block, "layer_rglru"
            mc = _rglru_consts(c_norm_g[j], c_w_in[j], c_b_in[j], c_conv_w[j], c_conv_b[j], c_w_a[j], c_b_a[j],
                               c_w_i[j], c_b_i[j], c_lambda[j], c_w_out[j])
            ms = _rglru_scratch(c_w_out.shape[1], c_conv_w.shape[1])
        else:
            block, name = _sconv_block, "layer_sconv"
            mc = _sconv_consts(d_norm_g[j], d_w_in[j], d_conv_w[j], d_w_out[j])
            ms = _sconv_scratch(d_w_out.shape[1], d_conv_w.shape[1])
        fc = _ffn_consts(ffn_norm_g, w_up, ffn_conv_w, ffn_conv_b, w_down)
        last = layer == depth - 1
        to_cast = [] if last else [(ffn_w_up, layer + 1), (ffn_w_down, layer + 1)]
        x, cast = _layer(x, block, mc, ms, fc, (layer, 0, layer, layer, 0), to_cast, final_norm_g,
                         first=(layer == 0), last=last, name=name)
        if cast:
            w_up, w_down = cast
    return x
```

```python
import functools

import jax
import jax.numpy as jnp
from jax import lax
from jax.experimental import pallas as pl
from jax.experimental.pallas import tpu as pltpu

EPS = 1e-6
SUB = 8
PB = 256
L = PB // SUB
C_GATE_C = 8.0
B_WINDOWS = (2, 4, 8, 16)
A_CHUNK = 128
FFN_CHUNK = 256
FFN_SLOTS = 4
VMEM_LIMIT = 56 * 1024 * 1024

F32 = jnp.float32
BF16 = jnp.bfloat16


def _to_perm(x):
    return x.reshape(SUB, L, x.shape[-1]).swapaxes(0, 1).reshape(x.shape)


def _from_perm(x):
    return x.reshape(L, SUB, x.shape[-1]).swapaxes(0, 1).reshape(x.shape)


def _block_time(shape, axis):
    p = lax.broadcasted_iota(jnp.int32, shape, axis)
    return (p & (SUB - 1)) * L + (p >> 3)


def _rms(x, g):
    ms = jnp.mean(x * x, axis=-1, keepdims=True)
    return x * lax.rsqrt(ms + EPS) * g


def _gelu(x):
    c = 0.7978845608028654
    half = 0.5 * x
    return half + half * jnp.tanh(x * (c + (c * 0.044715) * (x * x)))


def _dot(a, b):
    return jnp.dot(a, b, preferred_element_type=F32)


def _roll_tiles(t, k=1):
    parts = [pltpu.roll(t[r:r + SUB], k, 0) for r in range(0, t.shape[0], SUB)]
    return parts[0] if len(parts) == 1 else jnp.concatenate(parts, axis=0)


def _halo(cur_tail, prev_tail):
    first = (lax.broadcasted_iota(jnp.int32, cur_tail.shape, 0) & (SUB - 1)) == 0
    return jnp.where(first, _roll_tiles(prev_tail), _roll_tiles(cur_tail))


def _cat2(ref, a, b):
    return jnp.concatenate([ref[:, a], ref[:, b]], axis=1)


def _zero(*refs):
    for r in refs:
        r[...] = jnp.zeros(r.shape, r.dtype)


def _ffn_consts(g, w_up_bf16, conv_w, conv_b, w_down_bf16):
    depth, _, two_f = conv_w.shape
    return [g.reshape(depth, 1, g.shape[-1]), w_up_bf16, conv_w, conv_b.reshape(depth, 1, two_f), w_down_bf16]


def _ffn_scratch(f):
    hr = 2 * SUB
    return [pltpu.VMEM((FFN_SLOTS, PB + hr, 2 * FFN_CHUNK), F32), pltpu.VMEM((2, PB, f), BF16),
            pltpu.VMEM((hr, 2 * f), F32)]


def _ffn_block(x, n, blk, consts, scratch):
    g_ref, wup_ref, cw_ref, cb_ref, wdn_ref = [r.at[0] for r in consts]
    zbuf, ybuf, tail = scratch
    f, cw, hr = wdn_ref.shape[0], FFN_CHUNK, 2 * SUB
    h = _rms(x, g_ref[...]).astype(BF16)
    for c in range(f // cw):
        gs = slice(c * cw, (c + 1) * cw)
        vs = slice(f + c * cw, f + (c + 1) * cw)
        zs = slice(2 * c * cw, 2 * (c + 1) * cw)
        zb = zbuf.at[(n * (f // cw) + c) % FFN_SLOTS]
        zb[hr:hr + PB, :] = _dot(h, _cat2(wup_ref, gs, vs))
        cur = zb[PB:PB + hr, :]
        zb[0:hr, :] = _halo(cur, tail[:, zs])
        tail[:, zs] = cur
        w = _cat2(cw_ref, gs, vs)
        zc = (zb[0:PB, :] * w[0:1] + zb[SUB:SUB + PB, :] * w[1:2]
              + zb[hr:hr + PB, :] * w[2:3] + _cat2(cb_ref, gs, vs))
        gt = zc[:, :cw]
        ybuf[n % 2, :, gs] = ((gt * jax.nn.sigmoid(gt)) * zc[:, cw:]).astype(BF16)
        yield
    return x + _dot(ybuf[n % 2], wdn_ref[...])


def _gmlp_consts(g, w_in, b_in, vg, w_s, b_s, w_out):
    d, aw = w_out.shape[1], w_out.shape[0]
    p = jnp.arange(PB)
    idx = ((p % SUB) * L + p // SUB) % A_CHUNK
    return [g.reshape(1, d), w_in.astype(BF16), b_in.reshape(1, 2 * aw), vg.reshape(1, aw),
            w_s[:, idx][:, :, idx], b_s[:, idx][..., None], w_out.astype(BF16)]


def _gmlp_scratch(aw):
    return []


def _gmlp_block(x, n, blk, consts, scratch):
    g_ref, win_ref, bin_ref, vg_ref, ws_ref, bs_ref, wout_ref = consts
    aw, groups = wout_ref.shape[0], ws_ref.shape[0]
    gw = aw // groups
    tp = _block_time((PB, PB), 0)
    tq = _block_time((PB, PB), 1)
    chunk_bits = A_CHUNK.bit_length() - 1
    keep = (tq <= tp) & ((tq >> chunk_bits) == (tp >> chunk_bits))
    h = _rms(x, g_ref[...]).astype(BF16)
    zvs = []
    for gi in range(groups):
        vs = slice(aw + gi * gw, aw + (gi + 1) * gw)
        zvs.append(_gelu(_dot(h, win_ref[:, vs]) + bin_ref[:, vs]))
        yield
    v = _rms(jnp.concatenate(zvs, axis=1), vg_ref[...]).astype(BF16)
    yield
    ys = []
    for gi in range(groups):
        cs = slice(gi * gw, (gi + 1) * gw)
        ws = jnp.where(keep, ws_ref[gi], 0.0).astype(BF16)
        vm = _dot(ws, v[:, cs]) + bs_ref[gi]
        zu = _gelu(_dot(h, win_ref[:, cs]) + bin_ref[:, cs])
        ys.append((zu * vm).astype(BF16))
        yield
    return x + _dot(jnp.concatenate(ys, axis=1), wout_ref[...])


POOL_HALO = 16 * SUB


def _pool_consts(g, w_in, w_grp, b_grp, scale, w_out):
    d, bw = w_out.shape[1], w_out.shape[0]
    return [g.reshape(1, d), w_in.astype(BF16), w_grp, b_grp, scale.reshape(1, bw), w_out.astype(BF16)]


def _pool_scratch(bw):
    return [pltpu.VMEM((POOL_HALO, bw), F32)]


def _pool_block(x, n, blk, consts, scratch):
    g_ref, win_ref, wg_ref, bg_ref, sc_ref, wout_ref = consts
    (tail,) = scratch
    bw, groups = wout_ref.shape[0], wg_ref.shape[0]
    gw = bw // groups
    h = _rms(x, g_ref[...]).astype(BF16)
    z = _dot(h, win_ref[...])
    yield
    cur = z[PB - POOL_HALO:PB, :]
    ext = jnp.concatenate([_halo(cur, tail[...]), z], axis=0)
    tail[...] = cur
    pos = blk * PB + _block_time((PB, 1), 0)
    ys = []
    for gi, win in enumerate(B_WINDOWS):
        yield
        cs = slice(gi * gw, (gi + 1) * gw)
        s = ext[:, cs]
        step = 1
        while step < win:
            s = s[step * SUB:] + s[:-step * SUB]
            step *= 2
        acc = s[s.shape[0] - PB:]
        inv_cnt = 1.0 / jnp.minimum(pos + 1, win).astype(F32)
        p = (acc * inv_cnt - z[:, cs]).astype(BF16)
        ys.append(_dot(p, wg_ref[gi].astype(BF16)) + bg_ref[gi:gi + 1, :])
    yield
    y = jnp.concatenate(ys, axis=1) * sc_ref[...]
    return x + _dot(y.astype(BF16), wout_ref[...])


def _rglru_consts(g, w_in, b_in, conv_w, conv_b, w_a, b_a, w_i, b_i, lam, w_out):
    d, cwid = w_out.shape[1], w_out.shape[0]
    return [g.reshape(1, d), w_in.astype(BF16), b_in.reshape(1, 2 * cwid), conv_w, conv_b.reshape(1, cwid),
            w_a, b_a, w_i, b_i, lam.reshape(1, cwid), w_out.astype(BF16)]


def _rglru_scratch(cwid, taps):
    hr = (taps - 1) * SUB
    return [pltpu.VMEM((PB + hr, cwid), F32), pltpu.VMEM((PB, cwid), F32), pltpu.VMEM((PB, cwid), F32),
            pltpu.VMEM((hr, cwid), F32), pltpu.VMEM((SUB, cwid), F32)]


def _rglru_block(x, n, blk, consts, scratch):
    g_ref, win_ref, bin_ref, cw_ref, cb_ref, wa_ref, ba_ref, wi_ref, bi_ref, lam_ref, wout_ref = consts
    xbuf, abuf, hbuf, tail, hcar = scratch
    cwid, heads, taps = wout_ref.shape[0], wa_ref.shape[0], cw_ref.shape[0]
    hr = (taps - 1) * SUB
    hw = cwid // heads
    sub = lax.broadcasted_iota(jnp.int32, (SUB, cwid), 0)
    nlam = -lam_ref[...]
    softplus = jnp.maximum(nlam, 0.0) + jnp.log1p(jnp.exp(-jnp.abs(nlam)))
    h = _rms(x, g_ref[...]).astype(BF16)
    yield
    xbuf[hr:hr + PB, :] = _dot(h, win_ref[:, cwid:2 * cwid]) + bin_ref[:, cwid:2 * cwid]
    cur = xbuf[PB:PB + hr, :]
    xbuf[0:hr, :] = _halo(cur, tail[...])
    tail[...] = cur
    yield
    xr = cb_ref[...] + xbuf[hr:hr + PB, :] * cw_ref[taps - 1:taps, :]
    for k in range(1, taps):
        xr = xr + xbuf[hr - k * SUB:hr - k * SUB + PB, :] * cw_ref[taps - 1 - k:taps - k, :]
    xrb = xr.astype(BF16)
    yield
    a_cols, b_cols = [], []
    for hd in range(heads):
        cs = slice(hd * hw, (hd + 1) * hw)
        w2 = jnp.concatenate([wa_ref[hd], wi_ref[hd]], axis=1).astype(BF16)
        both = _dot(xrb[:, cs], w2)
        r = jax.nn.sigmoid(both[:, :hw] + ba_ref[hd:hd + 1, :])
        ig = jax.nn.sigmoid(both[:, hw:] + bi_ref[hd:hd + 1, :])
        log_a = (-C_GATE_C * r) * softplus[:, cs]
        a_cols.append(jnp.exp(log_a))
        th = jnp.tanh(log_a)
        b_cols.append(jnp.sqrt(-2.0 * th / (1.0 - th)) * (ig * xr[:, cs]))
        yield
    a = jnp.concatenate(a_cols, axis=1)
    b = jnp.concatenate(b_cols, axis=1)

    acum = a[0:SUB]
    hloc = b[0:SUB]
    abuf[0:SUB, :] = acum
    hbuf[0:SUB, :] = hloc
    for j in range(1, L):
        aj = a[j * SUB:(j + 1) * SUB]
        hloc = aj * hloc + b[j * SUB:(j + 1) * SUB]
        acum = aj * acum
        abuf[j * SUB:(j + 1) * SUB, :] = acum
        hbuf[j * SUB:(j + 1) * SUB, :] = hloc
        if j % SUB == SUB - 1:
            yield
    for k in (1, 2, 4):
        a_sh = jnp.where(sub < k, 1.0, pltpu.roll(acum, k, 0))
        h_sh = jnp.where(sub < k, 0.0, pltpu.roll(hloc, k, 0))
        hloc = acum * h_sh + hloc
        acum = acum * a_sh
    c0 = hcar[...]
    ends = hloc + acum * c0
    cin = jnp.where(sub == 0, c0, pltpu.roll(ends, 1, 0))
    hcar[...] = jnp.broadcast_to(ends[SUB - 1:SUB, :], (SUB, cwid))
    hs = hbuf[...] + abuf[...] * jnp.concatenate([cin] * L, axis=0)
    yield
    gate = _gelu(_dot(h, win_ref[:, 0:cwid]) + bin_ref[:, 0:cwid])
    return x + _dot((hs * gate).astype(BF16), wout_ref[...])


def _sconv_consts(g, w_in, conv_w, w_out):
    d = w_out.shape[1]
    return [g.reshape(1, d), w_in.astype(BF16), conv_w, w_out.astype(BF16)]


def _sconv_scratch(dw, taps):
    hr = (taps - 1) * SUB
    return [pltpu.VMEM((PB + hr, dw), F32), pltpu.VMEM((hr, dw), F32)]


def _sconv_block(x, n, blk, consts, scratch):
    g_ref, win_ref, cw_ref, wout_ref = consts
    mbuf, tail = scratch
    dw, taps = wout_ref.shape[0], cw_ref.shape[0]
    hr = (taps - 1) * SUB
    h = _rms(x, g_ref[...]).astype(BF16)
    yield
    cg = _dot(h, win_ref[:, dw:2 * dw])
    yield
    mbuf[hr:hr + PB, :] = cg * _dot(h, win_ref[:, 2 * dw:3 * dw])
    cur = mbuf[PB:PB + hr, :]
    mbuf[0:hr, :] = _halo(cur, tail[...])
    tail[...] = cur
    yield
    cv = mbuf[hr:hr + PB, :] * cw_ref[taps - 1:taps, :]
    for k in range(1, taps):
        cv = cv + mbuf[hr - k * SUB:hr - k * SUB + PB, :] * cw_ref[taps - 1 - k:taps - k, :]
    yield
    y = (_dot(h, win_ref[:, 0:dw]) * cv).astype(BF16)
    yield
    return x + _dot(y, wout_ref[...])


def _trace_alternately(gens):
    results = [None] * len(gens)
    live = [True] * len(gens)
    while any(live):
        for k in range(len(gens)):
            if live[k]:
                try:
                    next(gens[k])
                except StopIteration as done:
                    results[k], live[k] = done.value, False
    return results
def _layer_body(*refs, mixer_block, n_mix, n_mix_scr, n_cast, nb, nt, first, last):
    x_ref = refs[0]
    mix_consts = refs[1:1 + n_mix]
    ffn_consts = refs[1 + n_mix:1 + n_mix + 5]
    fg_ref = refs[1 + n_mix + 5]
    cast_src = refs[1 + n_mix + 6:1 + n_mix + 6 + n_cast]
    o_ref = refs[1 + n_mix + 6 + n_cast]
    cast_dst = refs[2 + n_mix + 6 + n_cast:2 + n_mix + 6 + 2 * n_cast]
    scratch = refs[2 + n_mix + 6 + 2 * n_cast:]
    mid, mix_scr, ffn_scr = scratch[0], scratch[1:1 + n_mix_scr], scratch[1 + n_mix_scr:]
    t = pl.program_id(0)

    for src, dst in zip(cast_src, cast_dst):
        dst[...] = src[...].astype(BF16)

    mix_tile = lax.rem(t, nt)
    ffn_tile = lax.rem(t + nt - 1, nt)

    @pl.when(t == 0)
    def _():
        _zero(mid)

    @pl.when(mix_tile == 0)
    def _():
        _zero(*mix_scr[len(mix_scr) - _N_CARRY[mixer_block]:])

    @pl.when((ffn_tile == 0) | (t == 0))
    def _():
        _zero(ffn_scr[-1])

    for n in range(nb):
        rows = slice(n * PB, (n + 1) * PB)
        x = x_ref[0, rows, :]
        if first:
            x = _to_perm(x)
        y, m = _trace_alternately([_ffn_block(mid[rows, :], n, ffn_tile * nb + n, ffn_consts, ffn_scr),
                                   mixer_block(x, n, mix_tile * nb + n, mix_consts, mix_scr)])
        if last:
            y = _from_perm(_rms(y, fg_ref[...]))
        o_ref[0, rows, :] = y
        mid[rows, :] = m


_N_CARRY = {_gmlp_block: 0, _pool_block: 1, _rglru_block: 2, _sconv_block: 1}


def _const_spec(shape):
    nd = len(shape)
    return pl.BlockSpec(shape, lambda t: (0,) * nd, pipeline_mode=pl.Buffered(1))


def _layer_spec(shape, layer):
    nd = len(shape)
    return pl.BlockSpec((1,) + tuple(shape[1:]), lambda t: (layer,) + (0,) * (nd - 1),
                        pipeline_mode=pl.Buffered(1))


BF16_ROWS = 16


def _layer(x, mixer_block, mix_consts, mix_scratch, ffn_consts, ffn_idx, to_cast, final_g, first, last, name,
           tm=512):
    bsz, s, d = x.shape
    f = ffn_consts[-1].shape[1]
    nt = s // tm
    tiles = bsz * nt
    consts = list(mix_consts) + list(ffn_consts) + [final_g.reshape(1, d)] + [w for w, _ in to_cast]
    const_specs = ([_const_spec(c.shape) for c in mix_consts]
                   + [_layer_spec(c.shape, k) for c, k in zip(ffn_consts, ffn_idx)] + [_const_spec((1, d))])
    cast_in, cast_out, cast_shapes = [], [], []
    for w, k in to_cast:
        n_rows, cols = w.shape[1:]
        rows = next(r for r in range(BF16_ROWS, n_rows + 1, BF16_ROWS)
                    if n_rows % r == 0 and n_rows // r <= tiles + 1)
        bands = n_rows // rows
        cast_in.append(pl.BlockSpec((1, rows, cols), lambda t, k=k, bands=bands: (k, jnp.minimum(t, bands - 1), 0)))
        cast_out.append(pl.BlockSpec((1, rows, cols), lambda t, bands=bands: (0, jnp.minimum(t, bands - 1), 0)))
        cast_shapes.append(jax.ShapeDtypeStruct((1, n_rows, cols), BF16))
    body = functools.partial(_layer_body, mixer_block=mixer_block, n_mix=len(mix_consts),
                             n_mix_scr=len(mix_scratch), n_cast=len(to_cast), nb=tm // PB, nt=nt,
                             first=first, last=last)

    def in_map(t):
        tt = jnp.minimum(t, tiles - 1)
        return (lax.div(tt, nt), lax.rem(tt, nt), 0)

    def out_map(t):
        tt = jnp.maximum(t - 1, 0)
        return (lax.div(tt, nt), lax.rem(tt, nt), 0)

    outs = pl.pallas_call(
        body,
        grid=(tiles + 1,),
        in_specs=[pl.BlockSpec((1, tm, d), in_map)] + const_specs + cast_in,
        out_specs=[pl.BlockSpec((1, tm, d), out_map)] + cast_out,
        out_shape=[jax.ShapeDtypeStruct(x.shape, x.dtype)] + cast_shapes,
        scratch_shapes=[pltpu.VMEM((tm, d), F32)] + list(mix_scratch) + _ffn_scratch(f),
        compiler_params=pltpu.CompilerParams(
            dimension_semantics=("arbitrary",), vmem_limit_bytes=VMEM_LIMIT),
        name=name,
    )(x, *consts)
    return outs[0], outs[1:]


def _round_bf16(stack, k, bands=8):
    _, n_rows, cols = stack.shape
    rows = n_rows // bands
    assert rows * bands == n_rows and rows % BF16_ROWS == 0

    def body(src, dst):
        dst[...] = src[...].astype(BF16)

    return pl.pallas_call(
        body,
        grid=(bands,),
        in_specs=[pl.BlockSpec((1, rows, cols), lambda i: (k, i, 0))],
        out_specs=pl.BlockSpec((1, rows, cols), lambda i: (0, i, 0)),
        out_shape=jax.ShapeDtypeStruct((1, n_rows, cols), BF16),
        compiler_params=pltpu.CompilerParams(dimension_semantics=("arbitrary",)),
        name="round_bf16",
    )(stack)


def kernel(x, a_norm_g, a_w_in, a_b_in, a_v_norm_g, a_w_s, a_b_s, a_w_out, b_norm_g, b_w_in, b_w_grp, b_b_grp, b_scale, b_w_out, c_norm_g, c_w_in, c_b_in, c_conv_w, c_conv_b, c_w_a, c_b_a, c_w_i, c_b_i, c_lambda, c_w_out, d_norm_g, d_w_in, d_conv_w, d_w_out, ffn_norm_g, ffn_w_up, ffn_conv_w, ffn_conv_b, ffn_w_down, final_norm_g):
    depth = ffn_norm_g.shape[0]
    n_mixers = 4
    w_up, w_down = _round_bf16(ffn_w_up, 0), _round_bf16(ffn_w_down, 0)
    for layer in range(depth):
        m, j = layer % n_mixers, layer // n_mixers
        if m == 0:
            block, name = _gmlp_block, "layer_gmlp"
            mc = _gmlp_consts(a_norm_g[j], a_w_in[j], a_b_in[j], a_v_norm_g[j], a_w_s[j], a_b_s[j], a_w_out[j])
            ms = _gmlp_scratch(a_w_out.shape[1])
        elif m == 1:
            block, name = _pool_block, "layer_pool"
            mc = _pool_consts(b_norm_g[j], b_w_in[j], b_w_grp[j], b_b_grp[j], b_scale[j], b_w_out[j])
            ms = _pool_scratch(b_w_out.shape[1])
        elif m == 2:
            block, name = _rglru_block, "layer_rglru"
            mc = _rglru_consts(c_norm_g[j], c_w_in[j], c_b_in[j], c_conv_w[j], c_conv_b[j], c_w_a[j], c_b_a[j],
                               c_w_i[j], c_b_i[j], c_lambda[j], c_w_out[j])
            ms = _rglru_scratch(c_w_out.shape[1], c_conv_w.shape[1])
        else:
            block, name = _sconv_block, "layer_sconv"
            mc = _sconv_consts(d_norm_g[j], d_w_in[j], d_conv_w[j], d_w_out[j])
            ms = _sconv_scratch(d_w_out.shape[1], d_conv_w.shape[1])
        fc = _ffn_consts(ffn_norm_g, w_up, ffn_conv_w, ffn_conv_b, w_down)
        last = layer == depth - 1
        to_cast = [] if last else [(ffn_w_up, layer + 1), (ffn_w_down, layer + 1)]
        x, cast = _layer(x, block, mc, ms, fc, (layer, 0, layer, layer, 0), to_cast, final_norm_g,
                         first=(layer == 0), last=last, name=name)
        if cast:
            w_up, w_down = cast
    return x
```
